```python
import functools
import jax, jax.numpy as jnp
from jax import lax
import numpy as np

D_MODEL = 4096
BATCH = 4
SEQ = 2048
DEPTH = 2
DEC_BATCH = 8
DEC_SEQ = 1
PAST_LEN = 16384
PAGE_SIZE = 128

SB_DH = 128
SB_W = D_MODEL // 2
SB_HEADS = SB_W // SB_DH
SB_QBLOCK = 128
SB_BIAS_LO = -8.0
SB_BIAS_HI = -4.0
HG_DK = 128
HG_DV = 128
HG_HEADS = (D_MODEL // 4) // HG_DV
ML_DK = 128
ML_DV = 128
ML_HEADS = (D_MODEL // 4) // ML_DV
ML_W = ML_HEADS * ML_DV
REC_CHUNK = 64
N_GROUPS = 4
EXPERTS_PER_GROUP = 8
N_EXPERTS = N_GROUPS * EXPERTS_PER_GROUP
TOP_K_IN_GROUP = 2
D_EXPERT = D_MODEL // 4
MOE_BLOCK = 128
ALPHA = (2 * DEPTH) ** 0.25
BETA = (8 * DEPTH) ** -0.25
LN_EPS = 1e-5
NORM_EPS = 1e-6

IN_WIDTHS = (SB_W, SB_W, SB_W,
             HG_HEADS * HG_DK, HG_HEADS * HG_DK,
             HG_HEADS * HG_DV, HG_HEADS * HG_DV,
             ML_HEADS * ML_DK, ML_HEADS * ML_DK, ML_W, ML_W,
             ML_HEADS, ML_HEADS,
             D_MODEL, D_MODEL, D_MODEL)
N_IN = sum(IN_WIDTHS)
IN_SPLITS = tuple(sum(IN_WIDTHS[:i]) for i in range(1, len(IN_WIDTHS)))

kernel_name = 'hybrid_sb_hgrn2_mlstm_hmoe_decode_step'


def _heads(t, n_heads):
    return t.reshape(t.shape[0], t.shape[1], n_heads, -1)


def layer_norm(x, g, b):
    xf = x.astype(jnp.float32)
    mu = jnp.mean(xf, axis=-1, keepdims=True)
    var = jnp.mean(jnp.square(xf - mu), axis=-1, keepdims=True)
    return ((xf - mu) * lax.rsqrt(var + LN_EPS) * g.astype(jnp.float32) + b.astype(jnp.float32)).astype(x.dtype)


def head_rms_norm(o, gain):
    o = o * lax.rsqrt(jnp.mean(o * o, axis=-1, keepdims=True) + NORM_EPS)
    return o.reshape(o.shape[0], o.shape[1], -1) * gain.astype(jnp.float32)


def stick_breaking_weights(z, causal):
    log_keep = jnp.where(causal, jax.nn.log_sigmoid(-z), 0.0)
    after = lax.cumsum(log_keep, axis=z.ndim - 1, reverse=True) - log_keep
    return jnp.where(causal, jnp.exp(jax.nn.log_sigmoid(z) + after), 0.0)


def sb_prompt(q, k, v, bias):
    b, t, h, d = q.shape
    blk = SB_QBLOCK if t % SB_QBLOCK == 0 else t
    nb = t // blk
    kpos = jnp.arange(t)
    qb = jnp.swapaxes(q.reshape(b, nb, blk, h, d), 0, 1)
    bias4 = bias.astype(jnp.float32)[None, :, None, None]

    def one_block(args):
        qi, i = args
        qpos = i * blk + jnp.arange(blk)
        z = jnp.einsum('bqhd,bkhd->bhqk', qi, k).astype(jnp.float32) * SB_DH ** -0.5 + bias4
        a = stick_breaking_weights(z, kpos[None, :] < qpos[:, None])
        return jnp.einsum('bhqk,bkhd->bqhd', a.astype(v.dtype), v)

    o = lax.map(one_block, (qb, jnp.arange(nb)))
    return jnp.swapaxes(o, 0, 1).reshape(b, t, h, d)


def sb_sample(q, k, v, bias, k_past, v_past):
    p_len, tq = k_past.shape[1], q.shape[1]
    z = (jnp.concatenate([jnp.einsum('bqhd,bkhd->bhqk', q, k_past),
                          jnp.einsum('bqhd,bkhd->bhqk', q, k)], axis=-1).astype(jnp.float32) * SB_DH ** -0.5
         + bias.astype(jnp.float32)[None, :, None, None])
    qpos = p_len + jnp.arange(tq)
    kpos = jnp.arange(p_len + tq)
    a = stick_breaking_weights(z, kpos[None, :] < qpos[:, None]).astype(v.dtype)
    return (jnp.einsum('bhqk,bkhd->bqhd', a[..., :p_len], v_past)
            + jnp.einsum('bhqk,bkhd->bqhd', a[..., p_len:], v))


def chunk_scan(step, carry, seqs):
    b, t = seqs[0].shape[:2]
    L = REC_CHUNK if t % REC_CHUNK == 0 else t
    nc = t // L
    xs = tuple(jnp.swapaxes(s.reshape((b, nc, L) + s.shape[2:]), 0, 1) for s in seqs)
    carry, ys = lax.scan(step, carry, xs)
    ys = jnp.swapaxes(ys, 0, 1)
    return carry, ys.reshape((b, t) + ys.shape[3:])


def hgrn2_chunk(S, inp):
    q, k, v, logf = inp
    L = q.shape[1]
    cf = jnp.cumsum(logf, axis=1)
    causal = jnp.tril(jnp.ones((L, L), bool))
    rel = jnp.where(causal[None, :, :, None, None], cf[:, :, None] - cf[:, None, :], -jnp.inf)
    scores = jnp.einsum('bthk,bshk,btshk->bths', q, k, jnp.exp(rel))
    o = (jnp.einsum('bths,bshv->bthv', scores, v)
         + jnp.einsum('bthk,bhkv->bthv', q * jnp.exp(cf), S))
    tot = cf[:, -1]
    S = jnp.exp(tot)[..., None] * S + jnp.einsum('bshk,bshv->bhkv', k * jnp.exp(tot[:, None] - cf), v)
    return S, o


def mlstm_chunk(carry, inp):
    C, n, m = carry
    q, k, v, li, lf = inp
    L = q.shape[1]
    cf = jnp.cumsum(lf, axis=1)
    causal = jnp.tril(jnp.ones((L, L), bool))
    logd = jnp.where(causal[None, :, :, None], cf[:, :, None] - cf[:, None, :] + li[:, None, :], -jnp.inf)
    a = cf + m[:, None]
    mt = jnp.maximum(a, jnp.max(logd, axis=2))
    w = jnp.exp(logd - mt[:, :, None]) * jnp.einsum('bthk,bshk->btsh', q, k)
    wi = jnp.exp(a - mt)
    num = jnp.einsum('btsh,bshv->bthv', w, v) + wi[..., None] * jnp.einsum('bthk,bhkv->bthv', q, C)
    den = jnp.sum(w, axis=2) + wi * jnp.einsum('bthk,bhk->bth', q, n)
    h = num / jnp.maximum(jnp.abs(den), jnp.exp(-mt))[..., None]
    tot = cf[:, -1]
    g = tot[:, None] - cf + li
    m_new = jnp.maximum(tot + m, jnp.max(g, axis=1))
    decay = jnp.exp(tot + m - m_new)
    ws = jnp.exp(g - m_new[:, None])
    C = decay[..., None, None] * C + jnp.einsum('bsh,bshk,bshv->bhkv', ws, k, v)
    n = decay[..., None] * n + jnp.einsum('bsh,bshk->bhk', ws, k)
    return (C, n, m_new), h


def token_mixer(x, l, P, sb_fn, hg_state0, ml_state0):
    f32 = jnp.float32
    dt = x.dtype
    b, t, _ = x.shape
    (sq, sk, sv, hq, hf, hi, hgate, mq, mk, mv, mo, mi, mf,
     g_sb, g_hg, g_ml) = jnp.split(x @ P['w_in'][l], IN_SPLITS, axis=-1)

    k_sb = _heads(sk, SB_HEADS)
    v_sb = _heads(sv, SB_HEADS)
    o_sb = sb_fn(_heads(sq, SB_HEADS), k_sb, v_sb, P['sb_bias'][l]).reshape(b, t, SB_W)

    lb_all = jnp.cumsum(jax.nn.softmax(P['hg_lb_logits'].astype(f32), axis=0), axis=0)
    lb = (lb_all - lb_all[0])[l]
    hf32 = hf.astype(f32)
    logf = jnp.logaddexp(jnp.log(lb), jnp.log1p(-lb) + jax.nn.log_sigmoid(hf32))
    k_hg = (1.0 - lb) * jax.nn.sigmoid(-hf32)
    q_hg = jax.nn.silu(hq.astype(f32)) * HG_DK ** -0.5
    s_hg, o_hg = chunk_scan(hgrn2_chunk, hg_state0.astype(f32),
                            (_heads(q_hg, HG_HEADS), _heads(k_hg, HG_HEADS),
                             _heads(hi.astype(f32), HG_HEADS), _heads(logf, HG_HEADS)))
    o_hg = head_rms_norm(o_hg, P['hg_norm'][l]) * jax.nn.silu(hgate.astype(f32))

    q_ml = _heads(mq.astype(f32), ML_HEADS)
    k_ml = _heads(mk.astype(f32), ML_HEADS) * ML_DK ** -0.5
    v_ml = _heads(mv.astype(f32), ML_HEADS)
    li = mi.astype(f32) + P['ml_bias_i'][l].astype(f32)
    lf = jax.nn.log_sigmoid(mf.astype(f32) + P['ml_bias_f'][l].astype(f32))
    c0, n0, m0 = ml_state0
    (c_ml, n_ml, m_ml), h_ml = chunk_scan(mlstm_chunk, (c0.astype(f32), n0.astype(f32), m0.astype(f32)),
                                          (q_ml, k_ml, v_ml, li, lf))
    o_ml = head_rms_norm(h_ml, P['ml_norm'][l]) * jax.nn.sigmoid(mo.astype(f32))

    merged = (jax.nn.sigmoid(g_sb) * (o_sb @ P['w_br_sb'][l])
              + jax.nn.sigmoid(g_hg) * (o_hg.astype(dt) @ P['w_br_hg'][l])
              + jax.nn.sigmoid(g_ml) * (o_ml.astype(dt) @ P['w_br_ml'][l]))
    return merged @ P['w_out'][l], (k_sb, v_sb, s_hg, c_ml, n_ml, m_ml)


def hier_route(h2, w_rg, b_rg, w_re, b_re):
    f32 = jnp.float32
    lg = (h2 @ w_rg).astype(f32) + b_rg.astype(f32)
    g_star = jnp.argmax(lg, axis=-1)
    p_g = jnp.take_along_axis(jax.nn.softmax(lg, axis=-1), g_star[:, None], axis=-1)
    le = ((h2 @ w_re).astype(f32) + b_re.astype(f32)).reshape(-1, N_GROUPS, EXPERTS_PER_GROUP)
    le_g = jnp.take_along_axis(le, g_star[:, None, None], axis=1)[:, 0]
    top_v, top_i = lax.top_k(le_g, TOP_K_IN_GROUP)
    gate = p_g * jax.nn.softmax(top_v, axis=-1)
    experts = (g_star[:, None] * EXPERTS_PER_GROUP + top_i).astype(jnp.int32)
    return experts, gate


def moe_ffn(h2, experts, gate, wg, wu, wd):
    n, dm = h2.shape
    m_assign = n * TOP_K_IN_GROUP
    blk = min(MOE_BLOCK, m_assign)
    nb = -(-m_assign // blk) + N_EXPERTS
    flat_e = experts.reshape(m_assign)
    flat_t = jnp.repeat(jnp.arange(n, dtype=jnp.int32), TOP_K_IN_GROUP)
    flat_w = gate.reshape(m_assign)
    order = jnp.argsort(flat_e)
    se = flat_e[order]
    counts = jnp.bincount(flat_e, length=N_EXPERTS)
    nblk_e = (counts + blk - 1) // blk
    blk_end = jnp.cumsum(nblk_e)
    blk_start = blk_end - nblk_e
    row_start = jnp.cumsum(counts) - counts
    dest = blk_start[se] * blk + (jnp.arange(m_assign) - row_start[se])
    slot_tok = jnp.full((nb * blk,), n, jnp.int32).at[dest].set(flat_t[order])
    slot_w = jnp.zeros((nb * blk,), jnp.float32).at[dest].set(flat_w[order])
    blk_e = jnp.clip(jnp.searchsorted(blk_end, jnp.arange(nb), side='right'), 0, N_EXPERTS - 1)
    xpad = jnp.concatenate([h2, jnp.zeros((1, dm), h2.dtype)], axis=0)
    xb = xpad[slot_tok].reshape(nb, blk, dm)

    def one_block(args):
        xi, e = args
        return (jax.nn.silu(xi @ wg[e]) * (xi @ wu[e])) @ wd[e]

    yb = lax.map(one_block, (xb, blk_e)).reshape(nb * blk, dm)
    out = jnp.zeros((n + 1, dm), yb.dtype).at[slot_tok].add(yb * slot_w[:, None].astype(yb.dtype))
    return out[:n]


def decoder_layer(x, l, P, sb_fn, hg_state0, ml_state0):
    mix, states = token_mixer(x, l, P, sb_fn, hg_state0, ml_state0)
    h = layer_norm(ALPHA * x + mix, P['ln1_g'][l], P['ln1_b'][l])
    b, t, dm = h.shape
    h2 = h.reshape(b * t, dm)
    experts, gate = hier_route(h2, P['w_router_group'][l], P['b_router_group'][l],
                               P['w_router_expert'][l], P['b_router_expert'][l])
    f = moe_ffn(h2, experts, gate, P['w_e_gate'][l], P['w_e_up'][l], P['w_e_down'][l]).reshape(b, t, dm)
    return layer_norm(ALPHA * h + f, P['ln2_g'][l], P['ln2_b'][l]), states


def setup_inputs(seed: int = 0) -> dict:
    key = jax.random.key(seed)
    ks = jax.random.split(key, 32)
    f32 = jnp.float32
    n_pages = PAST_LEN // PAGE_SIZE
    n_pool = (DEC_BATCH * n_pages * 5) // 4

    def nrm(k, shape, scale=1.0):
        return jax.random.normal(k, shape, f32) * scale

    page_table = jax.random.permutation(ks[4], n_pool)[:DEC_BATCH * n_pages].reshape(DEC_BATCH, n_pages).astype(jnp.int32)
    return {
        'x_prompt': nrm(ks[0], (BATCH, SEQ, D_MODEL)),
        'x_sample': nrm(ks[1], (DEC_BATCH, DEC_SEQ, D_MODEL)),
        'cache_k': nrm(ks[2], (DEPTH, n_pool, PAGE_SIZE, SB_HEADS, SB_DH)),
        'cache_v': nrm(ks[3], (DEPTH, n_pool, PAGE_SIZE, SB_HEADS, SB_DH)),
        'page_table': page_table,
        'state_hgrn': nrm(ks[5], (DEPTH, DEC_BATCH, HG_HEADS, HG_DK, HG_DV), 0.5),
        'state_mlstm_c': nrm(ks[6], (DEPTH, DEC_BATCH, ML_HEADS, ML_DK, ML_DV)),
        'state_mlstm_n': nrm(ks[7], (DEPTH, DEC_BATCH, ML_HEADS, ML_DK)),
        'state_mlstm_m': nrm(ks[8], (DEPTH, DEC_BATCH, ML_HEADS)),
        'w_in': nrm(ks[9], (DEPTH, D_MODEL, N_IN), D_MODEL ** -0.5),
        'sb_bias': jnp.linspace(SB_BIAS_LO, SB_BIAS_HI, SB_HEADS, dtype=f32) + nrm(ks[30], (DEPTH, SB_HEADS), 0.1),
        'hg_lb_logits': nrm(ks[10], (DEPTH, HG_HEADS * HG_DK), 0.5),
        'hg_norm': 1.0 + nrm(ks[11], (DEPTH, HG_HEADS * HG_DV), 0.02),
        'ml_bias_i': nrm(ks[12], (DEPTH, ML_HEADS), 0.1),
        'ml_bias_f': jnp.linspace(3.0, 6.0, ML_HEADS, dtype=f32) + nrm(ks[13], (DEPTH, ML_HEADS), 0.1),
        'ml_norm': 1.0 + nrm(ks[14], (DEPTH, ML_W), 0.02),
        'w_br_sb': nrm(ks[15], (DEPTH, SB_W, D_MODEL), BETA * SB_W ** -0.5),
        'w_br_hg': nrm(ks[16], (DEPTH, HG_HEADS * HG_DV, D_MODEL), BETA * (HG_HEADS * HG_DV) ** -0.5),
        'w_br_ml': nrm(ks[17], (DEPTH, ML_W, D_MODEL), BETA * ML_W ** -0.5),
        'w_out': nrm(ks[18], (DEPTH, D_MODEL, D_MODEL), BETA * D_MODEL ** -0.5),
        'ln1_g': 1.0 + nrm(ks[19], (DEPTH, D_MODEL), 0.02),
        'ln1_b': nrm(ks[20], (DEPTH, D_MODEL), 0.02),
        'w_router_group': nrm(ks[21], (DEPTH, D_MODEL, N_GROUPS), D_MODEL ** -0.5),
        'b_router_group': nrm(ks[22], (DEPTH, N_GROUPS), 0.01),
        'w_router_expert': nrm(ks[23], (DEPTH, D_MODEL, N_EXPERTS), D_MODEL ** -0.5),
        'b_router_expert': nrm(ks[24], (DEPTH, N_EXPERTS), 0.01),
        'w_e_gate': nrm(ks[25], (DEPTH, N_EXPERTS, D_MODEL, D_EXPERT), D_MODEL ** -0.5),
        'w_e_up': nrm(ks[26], (DEPTH, N_EXPERTS, D_MODEL, D_EXPERT), D_MODEL ** -0.5),
        'w_e_down': nrm(ks[27], (DEPTH, N_EXPERTS, D_EXPERT, D_MODEL), BETA * D_EXPERT ** -0.5),
        'ln2_g': 1.0 + nrm(ks[28], (DEPTH, D_MODEL), 0.02),
        'ln2_b': nrm(ks[29], (DEPTH, D_MODEL), 0.02),
    }


def reference(x_prompt, x_sample, cache_k, cache_v, page_table, state_hgrn, state_mlstm_c,
              state_mlstm_n, state_mlstm_m, w_in, sb_bias, hg_lb_logits, hg_norm, ml_bias_i, ml_bias_f,
              ml_norm, w_br_sb, w_br_hg, w_br_ml, w_out, ln1_g, ln1_b, w_router_group,
              b_router_group, w_router_expert, b_router_expert, w_e_gate, w_e_up, w_e_down,
              ln2_g, ln2_b):
    P = {'w_in': w_in, 'sb_bias': sb_bias, 'hg_lb_logits': hg_lb_logits, 'hg_norm': hg_norm,
         'ml_bias_i': ml_bias_i, 'ml_bias_f': ml_bias_f, 'ml_norm': ml_norm, 'w_br_sb': w_br_sb,
         'w_br_hg': w_br_hg, 'w_br_ml': w_br_ml, 'w_out': w_out, 'ln1_g': ln1_g, 'ln1_b': ln1_b,
         'w_router_group': w_router_group, 'b_router_group': b_router_group,
         'w_router_expert': w_router_expert, 'b_router_expert': b_router_expert,
         'w_e_gate': w_e_gate, 'w_e_up': w_e_up, 'w_e_down': w_e_down, 'ln2_g': ln2_g, 'ln2_b': ln2_b}
    f32 = jnp.float32
    bp, bs = x_prompt.shape[0], x_sample.shape[0]
    hg0 = jnp.zeros((bp, HG_HEADS, HG_DK, HG_DV), f32)
    ml0 = (jnp.zeros((bp, ML_HEADS, ML_DK, ML_DV), f32), jnp.zeros((bp, ML_HEADS, ML_DK), f32),
           jnp.zeros((bp, ML_HEADS), f32))
    yp, ys = x_prompt, x_sample
    prompt_states, sample_states = [], []
    for l in range(DEPTH):
        yp, st_p = decoder_layer(yp, l, P, sb_prompt, hg0, ml0)
        k_past = cache_k[l][page_table].reshape(bs, -1, SB_HEADS, SB_DH)
        v_past = cache_v[l][page_table].reshape(bs, -1, SB_HEADS, SB_DH)
        ys, st_s = decoder_layer(ys, l, P, functools.partial(sb_sample, k_past=k_past, v_past=v_past),
                                 state_hgrn[l], (state_mlstm_c[l], state_mlstm_n[l], state_mlstm_m[l]))
        prompt_states.append(st_p)
        sample_states.append(st_s)
    kp, vp, hp, cp, np_, mp = [jnp.stack(s) for s in zip(*prompt_states)]
    ks_, vs_, hs, cs, ns, ms = [jnp.stack(s) for s in zip(*sample_states)]
    return (yp, ys, kp, vp, ks_, vs_, hp, hs, cp, np_, mp, cs, ns, ms)
```

```python
import functools

import jax
import jax.numpy as jnp
from jax import lax
from jax.experimental import pallas as pl
from jax.experimental.pallas import tpu as pltpu

F32 = jnp.float32
BF16 = jnp.bfloat16
I32 = jnp.int32

LANES = 128
HEAD_DIM = 128
TAIL_ROWS = 128
VMEM_LIMIT = 56 * 1024 * 1024
LN_EPS = 1e-5
NORM_EPS = 1e-6
N_GROUPS = 4
EXPERTS_PER_GROUP = 8
N_EXPERTS = N_GROUPS * EXPERTS_PER_GROUP
MOE_ROWS = 256
HG_CHUNK = 16
ML_CHUNK = 128
NEG_INF = float("-inf")


def _params(*sem):
    return pltpu.CompilerParams(dimension_semantics=sem, vmem_limit_bytes=VMEM_LIMIT)


def _pick(n, candidates):
    for c in candidates:
        if n % c == 0:
            return c
    return n


def _softplus(x):
    return jnp.maximum(x, 0.0) + jnp.log1p(jnp.exp(-jnp.abs(x)))


def _sigmoid(x):
    return 1.0 / (1.0 + jnp.exp(-x))


def _dot(a, b):
    return jnp.dot(a, b, preferred_element_type=F32)


def _dot_nt(a, b):
    return lax.dot_general(a, b, (((1,), (1,)), ((), ())), preferred_element_type=F32)


def _dot_tn(a, b):
    return lax.dot_general(a, b, (((0,), (0,)), ((), ())), preferred_element_type=F32)


def _dot_f32(a, b):
    return jnp.dot(a, b, preferred_element_type=F32, precision=lax.Precision.HIGHEST)


def _mm_kernel(x_ref, w_ref, o_ref, wb_ref, *, act):
    @pl.when(pl.program_id(1) == 0)
    def _():
        wb_ref[...] = w_ref[...].astype(BF16)

    acc = _dot(x_ref[...], wb_ref[...])
    if act == "sigmoid":
        acc = _sigmoid(acc)
    o_ref[...] = acc.astype(o_ref.dtype)


def _matmul(x, w, *, lead=(), col0=0, ncols=None, out_dtype=F32, act=None, tn=512):
    m, k = x.shape
    ncols = w.shape[-1] - col0 if ncols is None else ncols
    tn = _pick(ncols, (tn, 256, 128))
    tm = _pick(m, (640, 512, 256, 128))
    assert col0 % tn == 0 and w.shape[-2] == k
    nlead = len(lead)
    w_spec = pl.BlockSpec((None,) * nlead + (k, tn), lambda j, i: tuple(lead) + (0, col0 // tn + j))
    return pl.pallas_call(
        functools.partial(_mm_kernel, act=act),
        grid=(ncols // tn, m // tm),
        in_specs=[pl.BlockSpec((tm, k), lambda j, i: (i, 0)), w_spec],
        out_specs=pl.BlockSpec((tm, tn), lambda j, i: (i, j)),
        out_shape=jax.ShapeDtypeStruct((m, ncols), out_dtype),
        scratch_shapes=[pltpu.VMEM((k, tn), BF16)],
        compiler_params=_params("arbitrary", "arbitrary"),
        name="dense_matmul",
    )(x, w)


def _merge_kernel(osb_ref, ohg_ref, oml_ref, g_sb_ref, g_hg_ref, g_ml_ref, wsb_ref, whg_ref, wml_ref,
                  o_ref, wsb_b, whg_b, wml_b):
    @pl.when(pl.program_id(1) == 0)
    def _():
        wsb_b[...] = wsb_ref[...].astype(BF16)
        whg_b[...] = whg_ref[...].astype(BF16)
        wml_b[...] = wml_ref[...].astype(BF16)

    merged = (g_sb_ref[...] * _dot(osb_ref[...], wsb_b[...])
              + g_hg_ref[...] * _dot(ohg_ref[...], whg_b[...])
              + g_ml_ref[...] * _dot(oml_ref[...], wml_b[...]))
    o_ref[...] = merged.astype(o_ref.dtype)


def _merge(o_sb, o_hg, o_ml, gates, w_sb, w_hg, w_ml, l):
    m = o_sb.shape[0]
    d = w_sb.shape[-1]
    tn = _pick(d, (512, 256, 128))
    tm = _pick(m, (640, 512, 256, 128))
    nj = d // tn

    def lhs(a):
        return pl.BlockSpec((tm, a.shape[1]), lambda j, i: (i, 0))

    def wspec(w):
        return pl.BlockSpec((None, w.shape[1], tn), lambda j, i: (l, 0, j))

    def gspec(which):
        return pl.BlockSpec((tm, tn), lambda j, i: (i, which * nj + j))

    return pl.pallas_call(
        _merge_kernel,
        grid=(nj, m // tm),
        in_specs=[lhs(o_sb), lhs(o_hg), lhs(o_ml), gspec(0), gspec(1), gspec(2),
                  wspec(w_sb), wspec(w_hg), wspec(w_ml)],
        out_specs=pl.BlockSpec((tm, tn), lambda j, i: (i, j)),
        out_shape=jax.ShapeDtypeStruct((m, d), BF16),
        scratch_shapes=[pltpu.VMEM((w_sb.shape[1], tn), BF16), pltpu.VMEM((w_hg.shape[1], tn), BF16),
                        pltpu.VMEM((w_ml.shape[1], tn), BF16)],
        compiler_params=_params("arbitrary", "arbitrary"),
        name="branch_merge",
    )(o_sb, o_hg, o_ml, gates, gates, gates, w_sb, w_hg, w_ml)


def _layer_norm_rows(z, g, b):
    mu = jnp.mean(z, axis=-1, keepdims=True)
    zc = z - mu
    var = jnp.mean(zc * zc, axis=-1, keepdims=True)
    return zc * lax.rsqrt(var + LN_EPS) * g + b


def _ln_router_kernel(x_ref, y_ref, g_ref, b_ref, wr_ref, h_ref, hb_ref, lg_ref, *, alpha):
    h = _layer_norm_rows(alpha * x_ref[...] + y_ref[...], g_ref[...], b_ref[...])
    h_ref[...] = h
    hb_ref[...] = h.astype(BF16)
    lg_ref[...] = _dot_f32(h, wr_ref[...])


def _ln_router(x, y, g, b, w_r, alpha):
    m, d = x.shape
    tm = _pick(m, (320, 256, 128))
    row = pl.BlockSpec((tm, d), lambda i: (i, 0))
    vec = pl.BlockSpec((1, d), lambda i: (0, 0))
    return pl.pallas_call(
        functools.partial(_ln_router_kernel, alpha=alpha),
        grid=(m // tm,),
        in_specs=[row, row, vec, vec, pl.BlockSpec((d, LANES), lambda i: (0, 0))],
        out_specs=[row, row, pl.BlockSpec((tm, LANES), lambda i: (i, 0))],
        out_shape=[jax.ShapeDtypeStruct((m, d), F32), jax.ShapeDtypeStruct((m, d), BF16),
                   jax.ShapeDtypeStruct((m, LANES), F32)],
        compiler_params=_params("arbitrary"),
        name="ln_router",
    )(x, y, g.reshape(1, d), b.reshape(1, d), w_r)


def _first_index_of_max(vals, lane):
    mx = jnp.max(vals, axis=-1, keepdims=True)
    idx = jnp.min(jnp.where(vals == mx, lane.astype(F32), float(LANES)), axis=-1, keepdims=True)
    return mx, idx.astype(I32)


def _route_kernel(lg_ref, b_ref, tok_i_ref, tok_w_ref, cnt_ref, run_ref):
    i = pl.program_id(0)

    @pl.when(i == 0)
    def _():
        run_ref[...] = jnp.zeros_like(run_ref)

    tb = lg_ref.shape[0]
    logits = lg_ref[...] + b_ref[...]
    lane = lax.broadcasted_iota(I32, (tb, LANES), 1)
    is_group = lane < N_GROUPS
    gl = jnp.where(is_group, logits, NEG_INF)
    gmax, g_star = _first_index_of_max(gl, lane)
    p_g = 1.0 / jnp.sum(jnp.where(is_group, jnp.exp(gl - gmax), 0.0), axis=-1, keepdims=True)

    e_lane = lane - N_GROUPS
    in_group = (e_lane >= g_star * EXPERTS_PER_GROUP) & (e_lane < (g_star + 1) * EXPERTS_PER_GROUP)
    v1, l1 = _first_index_of_max(jnp.where(in_group, logits, NEG_INF), lane)
    v2, l2 = _first_index_of_max(jnp.where(in_group & (lane != l1), logits, NEG_INF), lane)
    ex = jnp.exp(v2 - v1)
    w0 = p_g / (1.0 + ex)
    w1 = p_g * ex / (1.0 + ex)
    e0 = l1 - N_GROUPS
    e1 = l2 - N_GROUPS

    onehot = ((lane == e0) | (lane == e1)).astype(F32)
    row = lax.broadcasted_iota(I32, (tb, tb), 0)
    col = lax.broadcasted_iota(I32, (tb, tb), 1)
    before = (col < row).astype(BF16)
    prior = _dot(before, onehot.astype(BF16)) + run_ref[...]
    r0 = jnp.sum(jnp.where(lane == e0, prior, 0.0), axis=-1, keepdims=True).astype(I32)
    r1 = jnp.sum(jnp.where(lane == e1, prior, 0.0), axis=-1, keepdims=True).astype(I32)
    run_ref[...] = run_ref[...] + jnp.sum(onehot, axis=0, keepdims=True)

    tok_i_ref[...] = jnp.where(lane == 0, e0, jnp.where(lane == 1, e1, jnp.where(lane == 2, r0, r1)))
    tok_w_ref[...] = jnp.where(lane == 0, w0, w1)

    @pl.when(i == pl.num_programs(0) - 1)
    def _():
        cnt_ref[...] = jnp.broadcast_to(run_ref[...], cnt_ref.shape).astype(I32)


def _route(logits, bias_row):
    m = logits.shape[0]
    tb = _pick(m, (640, 512, 256, 128))
    blk = pl.BlockSpec((tb, LANES), lambda i: (i, 0))
    return pl.pallas_call(
        _route_kernel,
        grid=(m // tb,),
        in_specs=[blk, pl.BlockSpec((1, LANES), lambda i: (0, 0))],
        out_specs=[blk, blk, pl.BlockSpec((8, LANES), lambda i: (0, 0))],
        out_shape=[jax.ShapeDtypeStruct((m, LANES), I32), jax.ShapeDtypeStruct((m, LANES), F32),
                   jax.ShapeDtypeStruct((8, LANES), I32)],
        scratch_shapes=[pltpu.VMEM((1, LANES), F32)],
        compiler_params=_params("arbitrary"),
        name="route",
    )(logits, bias_row)


def _layout_kernel(cnt_ref, info_ref):
    cnt = cnt_ref[...].astype(F32)
    lane = lax.broadcasted_iota(I32, cnt.shape, 1)
    nblk = jnp.where(lane < N_EXPERTS, jnp.floor((cnt + (MOE_ROWS - 1)) * (1.0 / MOE_ROWS)), 0.0)
    src = lax.broadcasted_iota(I32, (LANES, LANES), 0)
    dst = lax.broadcasted_iota(I32, (LANES, LANES), 1)
    blk_end = _dot_f32(nblk, (src <= dst).astype(F32))
    blk_start = blk_end - nblk
    b_idx = lax.broadcasted_iota(I32, (LANES, LANES), 0).astype(F32)
    ended = (jnp.broadcast_to(blk_end[0:1, :], (LANES, LANES)) <= b_idx) & (dst < N_EXPERTS)
    owner = jnp.minimum(jnp.sum(ended.astype(F32), axis=-1, keepdims=True), float(N_EXPERTS - 1))
    info_ref[0:8, :] = blk_start.astype(I32)
    info_ref[8:16, :] = blk_end.astype(I32)
    info_ref[16:16 + LANES, :] = jnp.broadcast_to(owner, (LANES, LANES)).astype(I32)


def _layout(counts):
    return pl.pallas_call(
        _layout_kernel,
        out_shape=jax.ShapeDtypeStruct((16 + LANES, LANES), I32),
        name="moe_layout",
    )(counts)


def _gather_kernel(e0_ref, e1_ref, r0_ref, r1_ref, start_ref, nused_ref, h_ref, o_ref, slot_ref, buf_ref, sem):
    b = pl.program_id(0)
    n_tok = e0_ref.shape[0]
    rows = buf_ref.shape[0]

    @pl.when(b == 0)
    def _():
        def clear(s, c):
            slot_ref[s] = 0
            return c
        lax.fori_loop(0, slot_ref.shape[0], clear, 0)

        def place(t, c):
            slot_ref[start_ref[e0_ref[t]] * rows + r0_ref[t]] = t
            slot_ref[start_ref[e1_ref[t]] * rows + r1_ref[t]] = t
            return c
        lax.fori_loop(0, n_tok, place, 0)

    def copy(r):
        return pltpu.make_async_copy(h_ref.at[pl.ds(slot_ref[b * rows + r], 1), :], buf_ref.at[pl.ds(r, 1), :], sem)

    @pl.when(b < nused_ref[0])
    def _():
        def start(r, c):
            copy(r).start()
            return c
        lax.fori_loop(0, rows, start, 0)

        def wait(r, c):
            copy(r).wait()
            return c
        lax.fori_loop(0, rows, wait, 0)
        o_ref[...] = buf_ref[...].astype(BF16)

    @pl.when(b >= nused_ref[0])
    def _():
        o_ref[...] = jnp.zeros_like(o_ref)


def _gather_slots(h, e0, e1, r0, r1, blk_start, nused, n_blocks):
    n_tok, d = h.shape
    return pl.pallas_call(
        _gather_kernel,
        grid_spec=pltpu.PrefetchScalarGridSpec(
            num_scalar_prefetch=6,
            grid=(n_blocks,),
            in_specs=[pl.BlockSpec(memory_space=pl.ANY)],
            out_specs=pl.BlockSpec((MOE_ROWS, d), lambda b, *_: (b, 0)),
            scratch_shapes=[pltpu.SMEM((n_blocks * MOE_ROWS,), I32), pltpu.VMEM((MOE_ROWS, d), F32),
                            pltpu.SemaphoreType.DMA],
        ),
        out_shape=jax.ShapeDtypeStruct((n_blocks * MOE_ROWS, d), BF16),
        compiler_params=_params("arbitrary"),
        name="moe_gather",
    )(e0, e1, r0, r1, blk_start, nused, h)


def _expert_up_kernel(owner_ref, nused_ref, x_ref, wg_ref, wu_ref, o_ref, wg_b, wu_b):
    b = pl.program_id(1)
    prev = owner_ref[jnp.maximum(b - 1, 0)]

    @pl.when((b == 0) | (owner_ref[b] != prev))
    def _():
        wg_b[...] = wg_ref[...].astype(BF16)
        wu_b[...] = wu_ref[...].astype(BF16)

    @pl.when(b < nused_ref[0])
    def _():
        x = x_ref[...]
        g = _dot(x, wg_b[...])
        u = _dot(x, wu_b[...])
        o_ref[...] = (g * _sigmoid(g) * u).astype(BF16)

    @pl.when(b >= nused_ref[0])
    def _():
        o_ref[...] = jnp.zeros_like(o_ref)


def _expert_up(xs, w_gate, w_up, l, owner, nused):
    n_slots, d = xs.shape
    f = w_gate.shape[-1]
    tf = _pick(f, (512, 256, 128))
    n_blocks = n_slots // MOE_ROWS
    wspec = pl.BlockSpec((None, None, d, tf), lambda j, b, own, nu: (l, own[b], 0, j))
    return pl.pallas_call(
        _expert_up_kernel,
        grid_spec=pltpu.PrefetchScalarGridSpec(
            num_scalar_prefetch=2,
            grid=(f // tf, n_blocks),
            in_specs=[pl.BlockSpec((MOE_ROWS, d), lambda j, b, own, nu: (b, 0)), wspec, wspec],
            out_specs=pl.BlockSpec((MOE_ROWS, tf), lambda j, b, own, nu: (b, j)),
            scratch_shapes=[pltpu.VMEM((d, tf), BF16), pltpu.VMEM((d, tf), BF16)],
        ),
        out_shape=jax.ShapeDtypeStruct((n_slots, f), BF16),
        compiler_params=_params("arbitrary", "arbitrary"),
        name="expert_up",
    )(owner, nused, xs, w_gate, w_up)


def _expert_down_kernel(owner_ref, nused_ref, x_ref, w_ref, o_ref, w_b):
    b = pl.program_id(1)
    prev = owner_ref[jnp.maximum(b - 1, 0)]

    @pl.when((b == 0) | (owner_ref[b] != prev))
    def _():
        w_b[...] = w_ref[...].astype(BF16)

    @pl.when(b < nused_ref[0])
    def _():
        o_ref[...] = _dot(x_ref[...], w_b[...])

    @pl.when(b >= nused_ref[0])
    def _():
        o_ref[...] = jnp.zeros_like(o_ref)


def _expert_down(hmid, w_down, l, owner, nused):
    n_slots, f = hmid.shape
    d = w_down.shape[-1]
    tn = _pick(d, (1024, 512, 256, 128))
    n_blocks = n_slots // MOE_ROWS
    return pl.pallas_call(
        _expert_down_kernel,
        grid_spec=pltpu.PrefetchScalarGridSpec(
            num_scalar_prefetch=2,
            grid=(d // tn, n_blocks),
            in_specs=[pl.BlockSpec((MOE_ROWS, f), lambda j, b, own, nu: (b, 0)),
                      pl.BlockSpec((None, None, f, tn), lambda j, b, own, nu: (l, own[b], 0, j))],
            out_specs=pl.BlockSpec((MOE_ROWS, tn), lambda j, b, own, nu: (b, j)),
            scratch_shapes=[pltpu.VMEM((f, tn), BF16)],
        ),
        out_shape=jax.ShapeDtypeStruct((n_slots, d), F32),
        compiler_params=_params("arbitrary", "arbitrary"),
        name="expert_down",
    )(owner, nused, hmid, w_down)


def _combine_kernel(e0_ref, e1_ref, r0_ref, r1_ref, start_ref, h_ref, w_ref, g_ref, b_ref, y_ref,
                    o_ref, ob_ref, buf_ref, sem, *, alpha):
    i = pl.program_id(0)
    tb = h_ref.shape[0]

    def copies(r):
        t = i * tb + r
        s0 = start_ref[e0_ref[t]] * MOE_ROWS + r0_ref[t]
        s1 = start_ref[e1_ref[t]] * MOE_ROWS + r1_ref[t]
        return (pltpu.make_async_copy(y_ref.at[pl.ds(s0, 1), :], buf_ref.at[0, pl.ds(r, 1), :], sem),
                pltpu.make_async_copy(y_ref.at[pl.ds(s1, 1), :], buf_ref.at[1, pl.ds(r, 1), :], sem))

    def start(r, c):
        c0, c1 = copies(r)
        c0.start()
        c1.start()
        return c
    lax.fori_loop(0, tb, start, 0)

    def wait(r, c):
        c0, c1 = copies(r)
        c0.wait()
        c1.wait()
        return c
    lax.fori_loop(0, tb, wait, 0)

    w = w_ref[...]
    f = w[:, 0:1] * buf_ref[0] + w[:, 1:2] * buf_ref[1]
    out = _layer_norm_rows(alpha * h_ref[...] + f, g_ref[...], b_ref[...])
    o_ref[...] = out
    ob_ref[...] = out.astype(BF16)


def _combine(h, tok_w, g, b, y, e0, e1, r0, r1, blk_start, alpha):
    m, d = h.shape
    tb = 128
    row = lambda i, *_: (i, 0)
    vec = pl.BlockSpec((1, d), lambda i, *_: (0, 0))
    return pl.pallas_call(
        functools.partial(_combine_kernel, alpha=alpha),
        grid_spec=pltpu.PrefetchScalarGridSpec(
            num_scalar_prefetch=5,
            grid=(m // tb,),
            in_specs=[pl.BlockSpec((tb, d), row), pl.BlockSpec((tb, LANES), row), vec, vec,
                      pl.BlockSpec(memory_space=pl.ANY)],
            out_specs=[pl.BlockSpec((tb, d), row), pl.BlockSpec((tb, d), row)],
            scratch_shapes=[pltpu.VMEM((2, tb, d), F32), pltpu.SemaphoreType.DMA],
        ),
        out_shape=[jax.ShapeDtypeStruct((m, d), F32), jax.ShapeDtypeStruct((m, d), BF16)],
        compiler_params=_params("arbitrary"),
        name="moe_combine_ln",
    )(e0, e1, r0, r1, blk_start, h, tok_w, g.reshape(1, d), b.reshape(1, d), y)


def _sb_prompt_kernel(bias_ref, q_ref, k_ref, v_ref, o_ref, *, scale):
    h = pl.program_id(1)
    i = pl.program_id(2)
    tq = q_ref.shape[0]
    bias = bias_ref[h]
    q = q_ref[...].astype(BF16)
    row = lax.broadcasted_iota(I32, (tq, tq), 0)
    col = lax.broadcasted_iota(I32, (tq, tq), 1)
    later = (row > col).astype(BF16)

    def body(n, carry):
        run, acc = carry
        j = i - n
        off = pl.multiple_of(j * tq, tq)
        k = k_ref[pl.ds(off, tq), :].astype(BF16)
        v = v_ref[pl.ds(off, tq), :].astype(BF16)
        z = _dot_nt(q, k) * scale + bias
        valid = (col + j * tq) < (row + i * tq)
        t = jnp.log1p(jnp.exp(-jnp.abs(z)))
        log_keep = jnp.where(valid, -(jnp.maximum(z, 0.0) + t), 0.0)
        log_sig = -(jnp.maximum(-z, 0.0) + t)
        hi = log_keep.astype(BF16)
        lo = (log_keep - hi.astype(F32)).astype(BF16)
        after = _dot(hi, later) + _dot(lo, later)
        a = jnp.where(valid, jnp.exp(log_sig + after + run), 0.0)
        acc = acc + _dot(a.astype(BF16), v)
        run = run + jnp.sum(log_keep, axis=-1, keepdims=True)
        return run, acc

    _, acc = lax.fori_loop(0, i + 1, body, (jnp.zeros((tq, 1), F32), jnp.zeros((tq, HEAD_DIM), F32)))
    o_ref[...] = acc.astype(o_ref.dtype)


def _sb_prompt(proj, bias, batch, seq, heads):
    tq = _pick(seq, (256, 128))
    nq = seq // tq
    return pl.pallas_call(
        functools.partial(_sb_prompt_kernel, scale=HEAD_DIM ** -0.5),
        grid_spec=pltpu.PrefetchScalarGridSpec(
            num_scalar_prefetch=1,
            grid=(batch, heads, nq),
            in_specs=[pl.BlockSpec((tq, HEAD_DIM), lambda b, h, i, *_: (b * nq + i, h)),
                      pl.BlockSpec((seq, HEAD_DIM), lambda b, h, i, *_: (b, heads + h)),
                      pl.BlockSpec((seq, HEAD_DIM), lambda b, h, i, *_: (b, 2 * heads + h))],
            out_specs=pl.BlockSpec((tq, HEAD_DIM), lambda b, h, i, *_: (b * nq + i, h)),
        ),
        out_shape=jax.ShapeDtypeStruct((batch * seq, heads * HEAD_DIM), BF16),
        compiler_params=_params("arbitrary", "arbitrary", "arbitrary"),
        name="sb_prompt",
    )(bias, proj, proj, proj)


def _sb_sample_kernel(pt_ref, bias_ref, q_ref, kn_ref, vn_ref, k_ref, v_ref, o_ref, run_ref, acc_ref,
                      *, scale, past_len):
    p = pl.program_id(1)
    heads = q_ref.shape[0]
    page = k_ref.shape[0]
    width = page * heads
    q = q_ref[...]
    bias = bias_ref[...]

    @pl.when(p == 0)
    def _():
        z = jnp.sum(q * kn_ref[...], axis=-1, keepdims=True) * scale + bias
        valid = (lax.broadcasted_iota(I32, z.shape, 1) + past_len) < past_len
        t = jnp.log1p(jnp.exp(-jnp.abs(z)))
        log_keep = jnp.where(valid, -(jnp.maximum(z, 0.0) + t), 0.0)
        a = jnp.where(valid, jnp.exp(-(jnp.maximum(-z, 0.0) + t)), 0.0)
        run_ref[...] = log_keep
        acc_ref[...] = a * vn_ref[...]

    k2 = k_ref[...].reshape(width, HEAD_DIM).astype(BF16)
    v2 = v_ref[...].reshape(width, HEAD_DIM).astype(BF16)
    z = _dot_nt(q.astype(BF16), k2) * scale + bias
    lane = lax.broadcasted_iota(I32, (heads, width), 1)
    own = (lane % heads) == lax.broadcasted_iota(I32, (heads, width), 0)
    t = jnp.log1p(jnp.exp(-jnp.abs(z)))
    log_keep = jnp.where(own, -(jnp.maximum(z, 0.0) + t), 0.0)
    log_sig = -(jnp.maximum(-z, 0.0) + t)
    suffix = log_keep
    shift = heads
    while shift < width:
        moved = pltpu.roll(suffix, width - shift, axis=1)
        suffix = suffix + jnp.where(lane + shift < width, moved, 0.0)
        shift *= 2
    after = suffix - log_keep
    a = jnp.where(own, jnp.exp(log_sig + after + run_ref[...]), 0.0)
    acc_ref[...] = acc_ref[...] + _dot(a.astype(BF16), v2)
    run_ref[...] = run_ref[...] + jnp.sum(log_keep, axis=-1, keepdims=True)

    @pl.when(p == pl.num_programs(1) - 1)
    def _():
        o_ref[...] = acc_ref[...]


def _sb_sample(q, k_new, v_new, cache_k, cache_v, page_table, bias, l):
    nseq, heads, _ = q.shape
    n_pages = page_table.shape[1]
    page = cache_k.shape[2]

    def page_map(b, p, pt):
        return (l, pt[b * n_pages + (n_pages - 1 - p)], 0, 0, 0)

    head_blk = pl.BlockSpec((None, heads, HEAD_DIM), lambda b, p, pt: (b, 0, 0))
    page_blk = pl.BlockSpec((None, None, page, heads, HEAD_DIM), page_map)
    return pl.pallas_call(
        functools.partial(_sb_sample_kernel, scale=HEAD_DIM ** -0.5, past_len=n_pages * page),
        grid_spec=pltpu.PrefetchScalarGridSpec(
            num_scalar_prefetch=1,
            grid=(nseq, n_pages),
            in_specs=[pl.BlockSpec((heads, 1), lambda b, p, pt: (0, 0)), head_blk, head_blk, head_blk,
                      page_blk, page_blk],
            out_specs=head_blk,
            scratch_shapes=[pltpu.VMEM((heads, 1), F32), pltpu.VMEM((heads, HEAD_DIM), F32)],
        ),
        out_shape=jax.ShapeDtypeStruct((nseq, heads, HEAD_DIM), F32),
        compiler_params=_params("arbitrary", "arbitrary"),
        name="sb_sample",
    )(page_table.reshape(-1), bias.reshape(heads, 1), q, k_new, v_new, cache_k, cache_v)


def _hgrn_gates(hq, hf, log_lb, log_1m_lb, one_m_lb):
    a = log_lb
    b = log_1m_lb - _softplus(-hf)
    logf = jnp.maximum(a, b) + jnp.log1p(jnp.exp(-jnp.abs(a - b)))
    k = one_m_lb * _sigmoid(-hf)
    q = hq * _sigmoid(hq) * (HEAD_DIM ** -0.5)
    return q, k, logf


def _head_norm_gate(o, gain, gate):
    return o * lax.rsqrt(jnp.mean(o * o, axis=-1, keepdims=True) + NORM_EPS) * gain * gate


def _hgrn_prompt_kernel(hq_ref, hf_ref, hi_ref, hg_ref, llb_ref, l1m_ref, oml_ref, gain_ref,
                        o_ref, s_ref, cf_ref, st_ref):
    seq = hq_ref.shape[0]
    c = HG_CHUNK
    blk = 128
    r = lax.broadcasted_iota(I32, (blk, blk), 0)
    s = lax.broadcasted_iota(I32, (blk, blk), 1)
    tri = ((r // c == s // c) & (s <= r)).astype(F32)
    llb, l1m, oml = llb_ref[...], l1m_ref[...], oml_ref[...]

    def prefix(n, carry):
        off = pl.multiple_of(n * blk, blk)
        _, _, logf = _hgrn_gates(hq_ref[pl.ds(off, blk), :], hf_ref[pl.ds(off, blk), :], llb, l1m, oml)
        cf_ref[pl.ds(off, blk), :] = _dot_f32(tri, logf)
        return carry
    lax.fori_loop(0, seq // blk, prefix, 0)

    st_ref[...] = jnp.zeros_like(st_ref)
    t_idx = lax.broadcasted_iota(I32, (c, c, HEAD_DIM), 0)
    s_idx = lax.broadcasted_iota(I32, (c, c, HEAD_DIM), 1)
    causal = s_idx <= t_idx
    gain = gain_ref[...]

    def step(n, carry):
        off = pl.multiple_of(n * c, c)
        rows = pl.ds(off, c)
        q, k, _ = _hgrn_gates(hq_ref[rows, :], hf_ref[rows, :], llb, l1m, oml)
        v = hi_ref[rows, :]
        cf = cf_ref[rows, :]
        rel = jnp.where(causal, cf[:, None, :] - cf[None, :, :], 0.0)
        w = jnp.where(causal, jnp.exp(rel), 0.0)
        scores = jnp.sum(q[:, None, :] * k[None, :, :] * w, axis=-1, keepdims=True)
        o = jnp.sum(scores * v[None, :, :], axis=1)
        st = st_ref[...]
        o = o + _dot_nt((q * jnp.exp(cf)).astype(BF16), st.astype(BF16))
        tot = cf[c - 1:c, :]
        kd = (k * jnp.exp(tot - cf)).astype(BF16)
        st_ref[...] = jnp.exp(tot) * st + _dot_tn(v.astype(BF16), kd)
        g = hg_ref[rows, :]
        o_ref[rows, :] = _head_norm_gate(o, gain, g * _sigmoid(g)).astype(o_ref.dtype)
        return carry
    lax.fori_loop(0, seq // c, step, 0)
    s_ref[...] = st_ref[...].T


def _hgrn_prompt(proj, col0, log_lb, log_1m_lb, one_m_lb, gain, batch, seq, heads):
    cb = col0 // HEAD_DIM

    def col(which):
        return pl.BlockSpec((seq, HEAD_DIM), lambda b, h: (b, cb + which * heads + h))

    vec = pl.BlockSpec((None, 1, HEAD_DIM), lambda b, h: (h, 0, 0))
    return pl.pallas_call(
        _hgrn_prompt_kernel,
        grid=(batch, heads),
        in_specs=[col(0), col(1), col(2), col(3), vec, vec, vec, vec],
        out_specs=[pl.BlockSpec((seq, HEAD_DIM), lambda b, h: (b, h)),
                   pl.BlockSpec((None, None, HEAD_DIM, HEAD_DIM), lambda b, h: (b, h, 0, 0))],
        out_shape=[jax.ShapeDtypeStruct((batch * seq, heads * HEAD_DIM), BF16),
                   jax.ShapeDtypeStruct((batch, heads, HEAD_DIM, HEAD_DIM), F32)],
        scratch_shapes=[pltpu.VMEM((seq, HEAD_DIM), F32), pltpu.VMEM((HEAD_DIM, HEAD_DIM), F32)],
        compiler_params=_params("arbitrary", "arbitrary"),
        name="hgrn_prompt",
    )(proj, proj, proj, proj, log_lb, log_1m_lb, one_m_lb, gain)


def _to_column(row_vec):
    n = row_vec.shape[1]
    r = lax.broadcasted_iota(I32, (n, n), 0)
    c = lax.broadcasted_iota(I32, (n, n), 1)
    return jnp.sum(jnp.where(r == c, jnp.broadcast_to(row_vec, (n, n)), 0.0), axis=-1, keepdims=True)


def _hgrn_sample_kernel(hq_ref, hf_ref, hi_ref, hg_ref, llb_ref, l1m_ref, oml_ref, gain_ref, s0_ref,
                        o_ref, s_ref):
    q, k, logf = _hgrn_gates(hq_ref[...], hf_ref[...], llb_ref[...], l1m_ref[...], oml_ref[...])
    v = hi_ref[...]
    s_old = s0_ref[...]
    scores = jnp.sum(q * k, axis=-1, keepdims=True)
    qf_col = _to_column(q * jnp.exp(logf))
    o = scores * v + jnp.sum(qf_col * s_old, axis=0, keepdims=True)
    s_ref[...] = _to_column(jnp.exp(logf)) * s_old + _to_column(k) * v
    g = hg_ref[...]
    o_ref[...] = _head_norm_gate(o, gain_ref[...], g * _sigmoid(g))


def _hgrn_sample(parts, log_lb, log_1m_lb, one_m_lb, gain, state, l):
    _, nseq, heads = parts.shape[:3]

    def part(which):
        return pl.BlockSpec((None, None, None, 1, HEAD_DIM), lambda b, h: (which, b, h, 0, 0))

    vec = pl.BlockSpec((None, 1, HEAD_DIM), lambda b, h: (h, 0, 0))
    return pl.pallas_call(
        _hgrn_sample_kernel,
        grid=(nseq, heads),
        in_specs=[part(0), part(1), part(2), part(3), vec, vec, vec, vec,
                  pl.BlockSpec((None, None, None, HEAD_DIM, HEAD_DIM), lambda b, h: (l, b, h, 0, 0))],
        out_specs=[pl.BlockSpec((None, None, 1, HEAD_DIM), lambda b, h: (b, h, 0, 0)),
                   pl.BlockSpec((None, None, HEAD_DIM, HEAD_DIM), lambda b, h: (b, h, 0, 0))],
        out_shape=[jax.ShapeDtypeStruct((nseq, heads, 1, HEAD_DIM), F32),
                   jax.ShapeDtypeStruct((nseq, heads, HEAD_DIM, HEAD_DIM), F32)],
        compiler_params=_params("arbitrary", "arbitrary"),
        name="hgrn_sample",
    )(parts, parts, parts, parts, log_lb, log_1m_lb, one_m_lb, gain, state)


def _mlstm_prompt_kernel(bias_ref, q_ref, k_ref, v_ref, og_ref, gcol_ref, grow_ref, gain_ref,
                         o_ref, c_ref, n_ref, m_ref, cst_ref, nst_ref, mst_ref):
    h = pl.program_id(1)
    heads = pl.num_programs(1)
    seq = q_ref.shape[0]
    c = min(ML_CHUNK, seq)
    b_i = bias_ref[h]
    b_f = bias_ref[heads + h]
    cst_ref[...] = jnp.zeros_like(cst_ref)
    nst_ref[...] = jnp.zeros_like(nst_ref)
    mst_ref[...] = jnp.zeros_like(mst_ref)
    t_idx = lax.broadcasted_iota(I32, (c, c), 0)
    s_idx = lax.broadcasted_iota(I32, (c, c), 1)
    causal = s_idx <= t_idx
    gain = gain_ref[...]

    def step(n, carry):
        off = pl.multiple_of(n * c, c)
        rows = pl.ds(off, c)
        li_col = gcol_ref[0, rows, :] + b_i
        lf_col = -_softplus(-(gcol_ref[1, rows, :] + b_f))
        li_row = grow_ref[0, :, rows] + b_i
        lf_row = -_softplus(-(grow_ref[1, :, rows] + b_f))
        lf_rows = jnp.broadcast_to(lf_row, (c, c))
        lf_cols = jnp.broadcast_to(lf_col, (c, c))
        cf_col = jnp.sum(jnp.where(causal, lf_rows, 0.0), axis=-1, keepdims=True)
        cf_row = jnp.sum(jnp.where(t_idx <= s_idx, lf_cols, 0.0), axis=0, keepdims=True)
        m_old = mst_ref[...]
        logd = jnp.where(causal, cf_col - cf_row + li_row, NEG_INF)
        a_col = cf_col + m_old
        mt = jnp.maximum(a_col, jnp.max(logd, axis=-1, keepdims=True))
        q = q_ref[rows, :].astype(BF16)
        k = (k_ref[rows, :] * (HEAD_DIM ** -0.5))
        v = v_ref[rows, :].astype(BF16)
        w = jnp.where(causal, jnp.exp(logd - mt), 0.0) * _dot_nt(q, k.astype(BF16))
        wi = jnp.exp(a_col - mt)
        cst = cst_ref[...]
        nst = nst_ref[...]
        num = _dot(w.astype(BF16), v) + wi * _dot(q, cst.astype(BF16))
        den = jnp.sum(w, axis=-1, keepdims=True) + wi * jnp.sum(q_ref[rows, :] * nst, axis=-1, keepdims=True)
        hid = num / jnp.maximum(jnp.abs(den), jnp.exp(-mt))
        tot = jnp.sum(lf_row, axis=-1, keepdims=True)
        g_row = tot - cf_row + li_row
        g_col = tot - cf_col + li_col
        m_new = jnp.maximum(tot + m_old, jnp.max(g_row, axis=-1, keepdims=True))
        decay = jnp.exp(tot + m_old - m_new)
        wk = jnp.exp(g_col - m_new) * k
        cst_ref[...] = decay * cst + _dot_tn(wk.astype(BF16), v)
        nst_ref[...] = decay * nst + jnp.sum(wk, axis=0, keepdims=True)
        mst_ref[...] = m_new
        o_ref[rows, :] = _head_norm_gate(hid, gain, _sigmoid(og_ref[rows, :])).astype(o_ref.dtype)
        return carry
    lax.fori_loop(0, seq // c, step, 0)
    c_ref[...] = cst_ref[...]
    n_ref[...] = nst_ref[...]
    m_ref[...] = mst_ref[...]


def _mlstm_prompt(proj, col0, gate_cols, gate_rows, bias, gain, batch, seq, heads):
    cb = col0 // HEAD_DIM

    def col(which):
        return pl.BlockSpec((seq, HEAD_DIM), lambda b, h, *_: (b, cb + which * heads + h))

    return pl.pallas_call(
        _mlstm_prompt_kernel,
        grid_spec=pltpu.PrefetchScalarGridSpec(
            num_scalar_prefetch=1,
            grid=(batch, heads),
            in_specs=[col(0), col(1), col(2), col(3),
                      pl.BlockSpec((None, None, 2, seq, 1), lambda b, h, *_: (b, h, 0, 0, 0)),
                      pl.BlockSpec((None, None, 2, 1, seq), lambda b, h, *_: (b, h, 0, 0, 0)),
                      pl.BlockSpec((None, 1, HEAD_DIM), lambda b, h, *_: (h, 0, 0))],
            out_specs=[pl.BlockSpec((seq, HEAD_DIM), lambda b, h, *_: (b, h)),
                       pl.BlockSpec((None, None, HEAD_DIM, HEAD_DIM), lambda b, h, *_: (b, h, 0, 0)),
                       pl.BlockSpec((None, None, 1, HEAD_DIM), lambda b, h, *_: (b, h, 0, 0)),
                       pl.BlockSpec((None, None, 1, 1), lambda b, h, *_: (b, h, 0, 0))],
            scratch_shapes=[pltpu.VMEM((HEAD_DIM, HEAD_DIM), F32), pltpu.VMEM((1, HEAD_DIM), F32),
                            pltpu.VMEM((1, 1), F32)],
        ),
        out_shape=[jax.ShapeDtypeStruct((batch * seq, heads * HEAD_DIM), BF16),
                   jax.ShapeDtypeStruct((batch, heads, HEAD_DIM, HEAD_DIM), F32),
                   jax.ShapeDtypeStruct((batch, heads, 1, HEAD_DIM), F32),
                   jax.ShapeDtypeStruct((batch, heads, 1, 1), F32)],
        compiler_params=_params("arbitrary", "arbitrary"),
        name="mlstm_prompt",
    )(bias, proj, proj, proj, proj, gate_cols, gate_rows, gain)


def _mlstm_sample_kernel(bias_ref, q_ref, k_ref, v_ref, og_ref, gi_ref, gf_ref, gain_ref,
                         c0_ref, n0_ref, m0_ref, o_ref, c_ref, n_ref, m_ref):
    h = pl.program_id(1)
    heads = pl.num_programs(1)
    li = gi_ref[...] + bias_ref[h]
    lf = -_softplus(-(gf_ref[...] + bias_ref[heads + h]))
    q = q_ref[...]
    k = k_ref[...] * (HEAD_DIM ** -0.5)
    v = v_ref[...]
    c_old, n_old, m_old = c0_ref[...], n0_ref[...], m0_ref[...]
    a = lf + m_old
    mt = jnp.maximum(a, li)
    w = jnp.exp(li - mt) * jnp.sum(q * k, axis=-1, keepdims=True)
    wi = jnp.exp(a - mt)
    num = w * v + wi * jnp.sum(_to_column(q) * c_old, axis=0, keepdims=True)
    den = w + wi * jnp.sum(q * n_old, axis=-1, keepdims=True)
    hid = num / jnp.maximum(jnp.abs(den), jnp.exp(-mt))
    m_new = jnp.maximum(lf + m_old, li)
    decay = jnp.exp(lf + m_old - m_new)
    ws = jnp.exp(li - m_new)
    c_ref[...] = decay * c_old + _to_column(ws * k) * v
    n_ref[...] = decay * n_old + ws * k
    m_ref[...] = m_new
    o_ref[...] = _head_norm_gate(hid, gain_ref[...], _sigmoid(og_ref[...]))


def _mlstm_sample(parts, gate_i, gate_f, bias, gain, c0, n0, m0, l):
    _, nseq, heads = parts.shape[:3]

    def part(which):
        return pl.BlockSpec((None, None, None, 1, HEAD_DIM), lambda b, h, *_: (which, b, h, 0, 0))

    one = pl.BlockSpec((None, None, 1, 1), lambda b, h, *_: (b, h, 0, 0))
    vec_o = pl.BlockSpec((None, None, 1, HEAD_DIM), lambda b, h, *_: (b, h, 0, 0))
    mat_o = pl.BlockSpec((None, None, HEAD_DIM, HEAD_DIM), lambda b, h, *_: (b, h, 0, 0))
    return pl.pallas_call(
        _mlstm_sample_kernel,
        grid_spec=pltpu.PrefetchScalarGridSpec(
            num_scalar_prefetch=1,
            grid=(nseq, heads),
            in_specs=[part(0), part(1), part(2), part(3), one, one,
                      pl.BlockSpec((None, 1, HEAD_DIM), lambda b, h, *_: (h, 0, 0)),
                      pl.BlockSpec((None, None, None, HEAD_DIM, HEAD_DIM), lambda b, h, *_: (l, b, h, 0, 0)),
                      pl.BlockSpec((None, None, None, 1, HEAD_DIM), lambda b, h, *_: (l, b, h, 0, 0)),
                      pl.BlockSpec((None, None, None, 1, 1), lambda b, h, *_: (l, b, h, 0, 0))],
            out_specs=[vec_o, mat_o, vec_o, one],
        ),
        out_shape=[jax.ShapeDtypeStruct((nseq, heads, 1, HEAD_DIM), F32),
                   jax.ShapeDtypeStruct((nseq, heads, HEAD_DIM, HEAD_DIM), F32),
                   jax.ShapeDtypeStruct((nseq, heads, 1, HEAD_DIM), F32),
                   jax.ShapeDtypeStruct((nseq, heads, 1, 1), F32)],
        compiler_params=_params("arbitrary", "arbitrary"),
        name="mlstm_sample",
    )(bias, parts, parts, parts, parts, gate_i, gate_f, gain, c0, n0, m0)


def _with_tail(prompt_rows, sample_rows):
    ns = sample_rows.shape[0]
    pad = jnp.zeros((TAIL_ROWS - ns, sample_rows.shape[1]), sample_rows.dtype)
    return jnp.concatenate([prompt_rows, sample_rows, pad], axis=0)


def kernel(x_prompt, x_sample, cache_k, cache_v, page_table, state_hgrn, state_mlstm_c, state_mlstm_n,
           state_mlstm_m, w_in, sb_bias, hg_lb_logits, hg_norm, ml_bias_i, ml_bias_f, ml_norm, w_br_sb,
           w_br_hg, w_br_ml, w_out, ln1_g, ln1_b, w_router_group, b_router_group, w_router_expert,
           b_router_expert, w_e_gate, w_e_up, w_e_down, ln2_g, ln2_b):
    batch, seq, d_model = x_prompt.shape
    nseq = x_sample.shape[0]
    depth = w_in.shape[0]
    n_prompt = batch * seq
    sb_heads = sb_bias.shape[1]
    sb_w = sb_heads * HEAD_DIM
    hg_w = hg_norm.shape[1]
    hg_heads = hg_w // HEAD_DIM
    ml_w = ml_norm.shape[1]
    ml_heads = ml_w // HEAD_DIM
    col_hg = 3 * sb_w
    col_ml = col_hg + 4 * hg_w
    col_if = col_ml + 4 * ml_w
    col_gate = col_if + 2 * ml_heads
    assert x_sample.shape[1] == 1 and nseq <= TAIL_ROWS and w_in.shape[2] == col_gate + 3 * d_model
    alpha = (2 * depth) ** 0.25
    n_tok = n_prompt + TAIL_ROWS
    n_blocks = -(-2 * n_tok // MOE_ROWS) + N_EXPERTS
    assert n_blocks <= LANES

    lb_all = jnp.cumsum(jax.nn.softmax(hg_lb_logits.astype(F32), axis=0), axis=0)
    lower = lb_all - lb_all[0]

    x = _with_tail(x_prompt.reshape(n_prompt, d_model), x_sample.reshape(nseq, d_model))
    xb = x.astype(BF16)
    outs = {name: [] for name in ("kp", "vp", "ks", "vs", "hp", "hs", "cp", "np", "mp", "cs", "ns", "ms")}

    for l in range(depth):
        proj = _matmul(xb, w_in, lead=(l,), ncols=col_if)
        w_if = jnp.pad(w_in[l, :, col_if:col_gate], ((0, 0), (0, LANES - 2 * ml_heads)))
        gate_if = _matmul(xb, w_if)
        gates = _matmul(xb, w_in[l, :, col_gate:].astype(BF16), act="sigmoid")

        tail = proj[n_prompt:n_prompt + nseq]
        outs["kp"].append(proj[:n_prompt, sb_w:2 * sb_w].reshape(batch, seq, sb_heads, HEAD_DIM))
        outs["vp"].append(proj[:n_prompt, 2 * sb_w:3 * sb_w].reshape(batch, seq, sb_heads, HEAD_DIM))
        q_s = tail[:, :sb_w].reshape(nseq, sb_heads, HEAD_DIM)
        k_s = tail[:, sb_w:2 * sb_w].reshape(nseq, sb_heads, HEAD_DIM)
        v_s = tail[:, 2 * sb_w:3 * sb_w].reshape(nseq, sb_heads, HEAD_DIM)
        outs["ks"].append(k_s.reshape(nseq, 1, sb_heads, HEAD_DIM))
        outs["vs"].append(v_s.reshape(nseq, 1, sb_heads, HEAD_DIM))

        o_sb_p = _sb_prompt(proj, sb_bias[l], batch, seq, sb_heads)
        o_sb_s = _sb_sample(q_s, k_s, v_s, cache_k, cache_v, page_table, sb_bias[l], l)
        o_sb = _with_tail(o_sb_p, o_sb_s.reshape(nseq, sb_w).astype(BF16))

        lb = lower[l].reshape(hg_heads, 1, HEAD_DIM)
        log_lb, log_1m_lb, one_m_lb = jnp.log(lb), jnp.log1p(-lb), 1.0 - lb
        hg_gain = hg_norm[l].reshape(hg_heads, 1, HEAD_DIM)
        o_hg_p, s_p = _hgrn_prompt(proj, col_hg, log_lb, log_1m_lb, one_m_lb, hg_gain, batch, seq, hg_heads)
        hg_parts = tail[:, col_hg:col_ml].reshape(nseq, 4, hg_heads, 1, HEAD_DIM).swapaxes(0, 1)
        o_hg_s, s_s = _hgrn_sample(hg_parts, log_lb, log_1m_lb, one_m_lb, hg_gain, state_hgrn, l)
        o_hg = _with_tail(o_hg_p, o_hg_s.reshape(nseq, hg_w).astype(BF16))
        outs["hp"].append(s_p)
        outs["hs"].append(s_s)

        ml_bias = jnp.concatenate([ml_bias_i[l], ml_bias_f[l]]).astype(F32)
        ml_gain = ml_norm[l].reshape(ml_heads, 1, HEAD_DIM)
        g_p = gate_if[:n_prompt, :2 * ml_heads].reshape(batch, seq, 2, ml_heads).transpose(0, 3, 2, 1)
        o_ml_p, c_p, n_p, m_p = _mlstm_prompt(proj, col_ml, g_p[..., None], g_p[:, :, :, None, :], ml_bias,
                                              ml_gain, batch, seq, ml_heads)
        ml_parts = tail[:, col_ml:col_if].reshape(nseq, 4, ml_heads, 1, HEAD_DIM).swapaxes(0, 1)
        g_s = gate_if[n_prompt:n_prompt + nseq, :2 * ml_heads].reshape(nseq, 2, ml_heads, 1, 1)
        o_ml_s, c_s, n_s, m_s = _mlstm_sample(
            ml_parts, g_s[:, 0], g_s[:, 1], ml_bias, ml_gain, state_mlstm_c,
            state_mlstm_n.reshape(depth, nseq, ml_heads, 1, HEAD_DIM),
            state_mlstm_m.reshape(depth, nseq, ml_heads, 1, 1), l)
        o_ml = _with_tail(o_ml_p, o_ml_s.reshape(nseq, ml_w).astype(BF16))
        outs["cp"].append(c_p)
        outs["np"].append(n_p.reshape(batch, ml_heads, HEAD_DIM))
        outs["mp"].append(m_p.reshape(batch, ml_heads))
        outs["cs"].append(c_s)
        outs["ns"].append(n_s.reshape(nseq, ml_heads, HEAD_DIM))
        outs["ms"].append(m_s.reshape(nseq, ml_heads))

        merged = _merge(o_sb, o_hg, o_ml, gates, w_br_sb, w_br_hg, w_br_ml, l)
        mix = _matmul(merged, w_out, lead=(l,))
        w_r = jnp.pad(jnp.concatenate([w_router_group[l], w_router_expert[l]], axis=1).astype(F32),
                      ((0, 0), (0, LANES - N_GROUPS - N_EXPERTS)))
        b_r = jnp.pad(jnp.concatenate([b_router_group[l], b_router_expert[l]]).astype(F32),
                      (0, LANES - N_GROUPS - N_EXPERTS)).reshape(1, LANES)
        h, hb, logits = _ln_router(x, mix, ln1_g[l], ln1_b[l], w_r, alpha)

        tok_i, tok_w, counts = _route(logits, b_r)
        info = _layout(counts)
        blk_start = info[0, :N_EXPERTS]
        nused = info[8, N_EXPERTS - 1:N_EXPERTS]
        owner = info[16:16 + n_blocks, 0]
        e0, e1, r0, r1 = tok_i[:, 0], tok_i[:, 1], tok_i[:, 2], tok_i[:, 3]
        xs = _gather_slots(h, e0, e1, r0, r1, blk_start, nused, n_blocks)
        hmid = _expert_up(xs, w_e_gate, w_e_up, l, owner, nused)
        y = _expert_down(hmid, w_e_down, l, owner, nused)
        x, xb = _combine(h, tok_w, ln2_g[l], ln2_b[l], y, e0, e1, r0, r1, blk_start, alpha)

    st = {k: jnp.stack(v) for k, v in outs.items()}
    y_prompt = x[:n_prompt].reshape(batch, seq, d_model)
    y_sample = x[n_prompt:n_prompt + nseq].reshape(nseq, 1, d_model)
    return (y_prompt, y_sample, st["kp"], st["vp"], st["ks"], st["vs"], st["hp"], st["hs"],
            st["cp"], st["np"], st["mp"], st["cs"], st["ns"], st["ms"])
```

```python
import functools

import jax
import jax.numpy as jnp
from jax import lax
from jax.experimental import pallas as pl
from jax.experimental.pallas import tpu as pltpu

F32 = jnp.float32
BF16 = jnp.bfloat16
I32 = jnp.int32

LANES = 128
HEAD_DIM = 128
TAIL_ROWS = 128
VMEM_LIMIT = 56 * 1024 * 1024
LN_EPS = 1e-5
NORM_EPS = 1e-6
N_GROUPS = 4
EXPERTS_PER_GROUP = 8
N_EXPERTS = N_GROUPS * EXPERTS_PER_GROUP
MOE_ROWS = 256
HG_CHUNK = 16
ML_CHUNK = 128
NEG_INF = float("-inf")


def _params(*sem):
    return pltpu.CompilerParams(dimension_semantics=sem, vmem_limit_bytes=VMEM_LIMIT)


def _pick(n, candidates):
    for c in candidates:
        if n % c == 0:
            return c
    return n


def _softplus(x):
    return jnp.maximum(x, 0.0) + jnp.log1p(jnp.exp(-jnp.abs(x)))


def _sigmoid(x):
    return 1.0 / (1.0 + jnp.exp(-x))


def _dot(a, b):
    return jnp.dot(a, b, preferred_element_type=F32)


def _dot_nt(a, b):
    return lax.dot_general(a, b, (((1,), (1,)), ((), ())), preferred_element_type=F32)


def _dot_tn(a, b):
    return lax.dot_general(a, b, (((0,), (0,)), ((), ())), preferred_element_type=F32)


def _mxu_round(x):
    return x.astype(BF16).astype(F32)


def _dot_f32(a, b):
    return jnp.dot(a, b, preferred_element_type=F32, precision=lax.Precision.HIGHEST)


def _mm_kernel(x_ref, w_ref, o_ref, wb_ref, *, act):
    @pl.when(pl.program_id(1) == 0)
    def _():
        wb_ref[...] = w_ref[...].astype(BF16)

    acc = _dot(x_ref[...], wb_ref[...])
    if act == "sigmoid":
        acc = _sigmoid(acc)
    o_ref[...] = acc.astype(o_ref.dtype)


def _matmul(x, w, *, lead=(), col0=0, ncols=None, out_dtype=F32, act=None, tn=512):
    m, k = x.shape
    ncols = w.shape[-1] - col0 if ncols is None else ncols
    tn = _pick(ncols, (tn, 256, 128))
    tm = _pick(m, (640, 512, 256, 128))
    assert col0 % tn == 0 and w.shape[-2] == k
    nlead = len(lead)
    w_spec = pl.BlockSpec((None,) * nlead + (k, tn), lambda j, i: tuple(lead) + (0, col0 // tn + j))
    return pl.pallas_call(
        functools.partial(_mm_kernel, act=act),
        grid=(ncols // tn, m // tm),
        in_specs=[pl.BlockSpec((tm, k), lambda j, i: (i, 0)), w_spec],
        out_specs=pl.BlockSpec((tm, tn), lambda j, i: (i, j)),
        out_shape=jax.ShapeDtypeStruct((m, ncols), out_dtype),
        scratch_shapes=[pltpu.VMEM((k, tn), BF16)],
        compiler_params=_params("arbitrary", "arbitrary"),
        name="dense_matmul",
    )(x, w)


def _mm_t_kernel(x_ref, wt_ref, o_ref, wb_ref, *, act):
    @pl.when(pl.program_id(1) == 0)
    def _():
        wb_ref[...] = wt_ref[0].T.astype(BF16)

    acc = _dot(x_ref[...], wb_ref[...])
    if act == "sigmoid":
        acc = _sigmoid(acc)
    o_ref[...] = acc.astype(o_ref.dtype)


def _matmul_t(x, wt, l, *, row0, nrows, out_dtype=F32, act=None):
    m, k = x.shape
    tn = _pick(nrows, (512, 256, 128))
    tm = _pick(m, (640, 512, 256, 128))
    assert wt.shape[2] == k and row0 % 8 == 0
    if row0 % tn == 0:
        w_spec = pl.BlockSpec((1, tn, k), lambda j, i: (l, row0 // tn + j, 0))
    else:
        w_spec = pl.BlockSpec((pl.Element(1), pl.Element(tn), pl.Element(k)),
                              lambda j, i: (l, pl.multiple_of(row0 + j * tn, 8), 0))
    return pl.pallas_call(
        functools.partial(_mm_t_kernel, act=act),
        grid=(nrows // tn, m // tm),
        in_specs=[pl.BlockSpec((tm, k), lambda j, i: (i, 0)), w_spec],
        out_specs=pl.BlockSpec((tm, tn), lambda j, i: (i, j)),
        out_shape=jax.ShapeDtypeStruct((m, nrows), out_dtype),
        scratch_shapes=[pltpu.VMEM((k, tn), BF16)],
        compiler_params=_params("arbitrary", "arbitrary"),
        name="dense_matmul_t",
    )(x, wt)


def _merge_kernel(osb_ref, ohg_ref, oml_ref, g_sb_ref, g_hg_ref, g_ml_ref, wsb_ref, whg_ref, wml_ref,
                  o_ref, wsb_b, whg_b, wml_b):
    @pl.when(pl.program_id(1) == 0)
    def _():
        wsb_b[...] = wsb_ref[...].astype(BF16)
        whg_b[...] = whg_ref[...].astype(BF16)
        wml_b[...] = wml_ref[...].astype(BF16)

    merged = (g_sb_ref[...] * _dot(osb_ref[...], wsb_b[...])
              + g_hg_ref[...] * _dot(ohg_ref[...], whg_b[...])
              + g_ml_ref[...] * _dot(oml_ref[...], wml_b[...]))
    o_ref[...] = merged.astype(o_ref.dtype)


def _merge(o_sb, o_hg, o_ml, gates, w_sb, w_hg, w_ml, l):
    m = o_sb.shape[0]
    d = w_sb.shape[-1]
    tn = _pick(d, (512, 256, 128))
    tm = _pick(m, (640, 512, 256, 128))
    nj = d // tn

    def lhs(a):
        return pl.BlockSpec((tm, a.shape[1]), lambda j, i: (i, 0))

    def wspec(w):
        return pl.BlockSpec((None, w.shape[1], tn), lambda j, i: (l, 0, j))

    def gspec(which):
        return pl.BlockSpec((tm, tn), lambda j, i: (i, which * nj + j))

    return pl.pallas_call(
        _merge_kernel,
        grid=(nj, m // tm),
        in_specs=[lhs(o_sb), lhs(o_hg), lhs(o_ml), gspec(0), gspec(1), gspec(2),
                  wspec(w_sb), wspec(w_hg), wspec(w_ml)],
        out_specs=pl.BlockSpec((tm, tn), lambda j, i: (i, j)),
        out_shape=jax.ShapeDtypeStruct((m, d), BF16),
        scratch_shapes=[pltpu.VMEM((w_sb.shape[1], tn), BF16), pltpu.VMEM((w_hg.shape[1], tn), BF16),
                        pltpu.VMEM((w_ml.shape[1], tn), BF16)],
        compiler_params=_params("arbitrary", "arbitrary"),
        name="branch_merge",
    )(o_sb, o_hg, o_ml, gates, gates, gates, w_sb, w_hg, w_ml)


def _layer_norm_rows(z, g, b):
    mu = jnp.mean(z, axis=-1, keepdims=True)
    zc = z - mu
    var = jnp.mean(zc * zc, axis=-1, keepdims=True)
    return zc * lax.rsqrt(var + LN_EPS) * g + b


def _ln_router_kernel(x_ref, y_ref, g_ref, b_ref, wr_ref, h_ref, lg_ref, *, alpha):
    h = _layer_norm_rows(alpha * x_ref[...] + y_ref[...], g_ref[...], b_ref[...])
    h_ref[...] = h
    lg_ref[...] = _dot(h.astype(BF16), wr_ref[...].astype(BF16))


def _ln_router(x, y, g, b, w_r, alpha):
    m, d = x.shape
    tm = _pick(m, (320, 256, 128))
    row = pl.BlockSpec((tm, d), lambda i: (i, 0))
    vec = pl.BlockSpec((1, d), lambda i: (0, 0))
    return pl.pallas_call(
        functools.partial(_ln_router_kernel, alpha=alpha),
        grid=(m // tm,),
        in_specs=[row, row, vec, vec, pl.BlockSpec((d, LANES), lambda i: (0, 0))],
        out_specs=[row, pl.BlockSpec((tm, LANES), lambda i: (i, 0))],
        out_shape=[jax.ShapeDtypeStruct((m, d), F32), jax.ShapeDtypeStruct((m, LANES), F32)],
        compiler_params=_params("arbitrary"),
        name="ln_router",
    )(x, y, g.reshape(1, d), b.reshape(1, d), w_r)


def _first_index_of_max(vals, lane):
    mx = jnp.max(vals, axis=-1, keepdims=True)
    idx = jnp.min(jnp.where(vals == mx, lane.astype(F32), float(LANES)), axis=-1, keepdims=True)
    return mx, idx.astype(I32)


def _route_kernel(lg_ref, b_ref, tok_i_ref, tok_w_ref, cnt_ref, run_ref):
    i = pl.program_id(0)

    @pl.when(i == 0)
    def _():
        run_ref[...] = jnp.zeros_like(run_ref)

    tb = lg_ref.shape[0]
    logits = lg_ref[...] + b_ref[...]
    lane = lax.broadcasted_iota(I32, (tb, LANES), 1)
    is_group = lane < N_GROUPS
    gl = jnp.where(is_group, logits, NEG_INF)
    gmax, g_star = _first_index_of_max(gl, lane)
    p_g = 1.0 / jnp.sum(jnp.where(is_group, jnp.exp(gl - gmax), 0.0), axis=-1, keepdims=True)

    e_lane = lane - N_GROUPS
    in_group = (e_lane >= g_star * EXPERTS_PER_GROUP) & (e_lane < (g_star + 1) * EXPERTS_PER_GROUP)
    v1, l1 = _first_index_of_max(jnp.where(in_group, logits, NEG_INF), lane)
    v2, l2 = _first_index_of_max(jnp.where(in_group & (lane != l1), logits, NEG_INF), lane)
    ex = jnp.exp(v2 - v1)
    w0 = p_g / (1.0 + ex)
    w1 = p_g * ex / (1.0 + ex)
    e0 = l1 - N_GROUPS
    e1 = l2 - N_GROUPS

    onehot = ((lane == e0) | (lane == e1)).astype(F32)
    row = lax.broadcasted_iota(I32, (tb, tb), 0)
    col = lax.broadcasted_iota(I32, (tb, tb), 1)
    before = (col < row).astype(BF16)
    prior = _dot(before, onehot.astype(BF16)) + run_ref[...]
    r0 = jnp.sum(jnp.where(lane == e0, prior, 0.0), axis=-1, keepdims=True).astype(I32)
    r1 = jnp.sum(jnp.where(lane == e1, prior, 0.0), axis=-1, keepdims=True).astype(I32)
    run_ref[...] = run_ref[...] + jnp.sum(onehot, axis=0, keepdims=True)

    tok_i_ref[...] = jnp.where(lane == 0, e0, jnp.where(lane == 1, e1, jnp.where(lane == 2, r0, r1)))
    tok_w_ref[...] = jnp.where(lane == 0, w0, w1)

    @pl.when(i == pl.num_programs(0) - 1)
    def _():
        cnt_ref[...] = jnp.broadcast_to(run_ref[...], cnt_ref.shape).astype(I32)


def _route(logits, bias_row):
    m = logits.shape[0]
    tb = _pick(m, (640, 512, 256, 128))
    blk = pl.BlockSpec((tb, LANES), lambda i: (i, 0))
    return pl.pallas_call(
        _route_kernel,
        grid=(m // tb,),
        in_specs=[blk, pl.BlockSpec((1, LANES), lambda i: (0, 0))],
        out_specs=[blk, blk, pl.BlockSpec((8, LANES), lambda i: (0, 0))],
        out_shape=[jax.ShapeDtypeStruct((m, LANES), I32), jax.ShapeDtypeStruct((m, LANES), F32),
                   jax.ShapeDtypeStruct((8, LANES), I32)],
        scratch_shapes=[pltpu.VMEM((1, LANES), F32)],
        compiler_params=_params("arbitrary"),
        name="route",
    )(logits, bias_row)


def _layout_kernel(cnt_ref, info_ref):
    cnt = cnt_ref[...].astype(F32)
    lane = lax.broadcasted_iota(I32, cnt.shape, 1)
    nblk = jnp.where(lane < N_EXPERTS, jnp.floor((cnt + (MOE_ROWS - 1)) * (1.0 / MOE_ROWS)), 0.0)
    src = lax.broadcasted_iota(I32, (LANES, LANES), 0)
    dst = lax.broadcasted_iota(I32, (LANES, LANES), 1)
    blk_end = _dot_f32(nblk, (src <= dst).astype(F32))
    blk_start = blk_end - nblk
    b_idx = lax.broadcasted_iota(I32, (LANES, LANES), 0).astype(F32)
    ended = (jnp.broadcast_to(blk_end[0:1, :], (LANES, LANES)) <= b_idx) & (dst < N_EXPERTS)
    owner = jnp.minimum(jnp.sum(ended.astype(F32), axis=-1, keepdims=True), float(N_EXPERTS - 1))
    info_ref[0:8, :] = blk_start.astype(I32)
    info_ref[8:16, :] = blk_end.astype(I32)
    info_ref[16:16 + LANES, :] = jnp.broadcast_to(owner, (LANES, LANES)).astype(I32)


def _layout(counts):
    return pl.pallas_call(
        _layout_kernel,
        out_shape=jax.ShapeDtypeStruct((16 + LANES, LANES), I32),
        name="moe_layout",
    )(counts)


def _gather_kernel(e0_ref, e1_ref, r0_ref, r1_ref, start_ref, cnt_ref, nused_ref, h_ref, o_ref,
                   slot_ref, buf_ref, sem):
    b = pl.program_id(0)
    n_tok = e0_ref.shape[0]
    rows = buf_ref.shape[1]
    nused = nused_ref[0]

    def copy(blk, r):
        slot = blk % 2
        return pltpu.make_async_copy(h_ref.at[pl.ds(slot_ref[blk * rows + r], 1), :],
                                     buf_ref.at[slot, pl.ds(r, 1), :], sem.at[slot])

    def issue(blk):
        def start(r, c):
            copy(blk, r).start()
            return c
        lax.fori_loop(0, rows, start, 0, unroll=8)

    @pl.when(b == 0)
    def _():
        def pad_expert(e, c):
            base = start_ref[e] * rows
            used = cnt_ref[e]

            def pad(s, c2):
                slot_ref[base + s] = 0
                return c2
            return lax.fori_loop(used, (used + rows - 1) // rows * rows, pad, c)
        lax.fori_loop(0, N_EXPERTS, pad_expert, 0)

        def place(t, c):
            slot_ref[start_ref[e0_ref[t]] * rows + r0_ref[t]] = t
            slot_ref[start_ref[e1_ref[t]] * rows + r1_ref[t]] = t
            return c
        lax.fori_loop(0, n_tok, place, 0)

        @pl.when(nused > 0)
        def _():
            issue(0)

    @pl.when(b + 1 < nused)
    def _():
        issue(b + 1)

    @pl.when(b < nused)
    def _():
        def wait(r, c):
            copy(b, r).wait()
            return c
        lax.fori_loop(0, rows, wait, 0, unroll=8)
        o_ref[...] = buf_ref[b % 2].astype(BF16)

    @pl.when(b >= nused)
    def _():
        o_ref[...] = jnp.zeros_like(o_ref)


def _gather_slots(h, e0, e1, r0, r1, blk_start, counts, nused, n_blocks):
    n_tok, d = h.shape
    return pl.pallas_call(
        _gather_kernel,
        grid_spec=pltpu.PrefetchScalarGridSpec(
            num_scalar_prefetch=7,
            grid=(n_blocks,),
            in_specs=[pl.BlockSpec(memory_space=pl.ANY)],
            out_specs=pl.BlockSpec((MOE_ROWS, d), lambda b, *_: (b, 0)),
            scratch_shapes=[pltpu.SMEM((n_blocks * MOE_ROWS,), I32), pltpu.VMEM((2, MOE_ROWS, d), F32),
                            pltpu.SemaphoreType.DMA((2,))],
        ),
        out_shape=jax.ShapeDtypeStruct((n_blocks * MOE_ROWS, d), BF16),
        compiler_params=_params("arbitrary"),
        name="moe_gather",
    )(e0, e1, r0, r1, blk_start, counts, nused, h)


def _expert_weights_step(owner_ref, start_ref, end_ref, nused, w_hbm, staging, resident, sems, l):
    j = pl.program_id(0)
    b = pl.program_id(1)
    e = owner_ref[b]
    tile = resident[0].shape[1]

    def fetch(jj, ee):
        cols = pl.ds(pl.multiple_of(jj * tile, tile), tile)
        return [pltpu.make_async_copy(w.at[l, ee, :, cols], st, sems.at[n])
                for n, (w, st) in enumerate(zip(w_hbm, staging))]

    @pl.when((b < nused) & (b == start_ref[e]))
    def _():
        @pl.when((j == 0) & (b == 0))
        def _():
            for c in fetch(j, e):
                c.start()
        for c in fetch(j, e):
            c.wait()
        chunk = _pick(resident[0].shape[0], (256, 128, 16))

        def cast(n, c):
            rows = pl.ds(pl.multiple_of(n * chunk, chunk), chunk)
            for st, res in zip(staging, resident):
                res[rows, :] = st[rows, :].astype(BF16)
            return c
        lax.fori_loop(0, resident[0].shape[0] // chunk, cast, 0)
        nxt = end_ref[e]

        @pl.when(nxt < nused)
        def _():
            for c in fetch(j, owner_ref[jnp.minimum(nxt, owner_ref.shape[0] - 1)]):
                c.start()

        @pl.when((nxt >= nused) & (j + 1 < pl.num_programs(0)))
        def _():
            for c in fetch(j + 1, owner_ref[0]):
                c.start()


def _expert_up_kernel(owner_ref, start_ref, end_ref, nused_ref, x_ref, wg_hbm, wu_hbm, o_ref,
                      wg_st, wu_st, wg_b, wu_b, sems, *, l):
    b = pl.program_id(1)
    nused = nused_ref[0]
    _expert_weights_step(owner_ref, start_ref, end_ref, nused, (wg_hbm, wu_hbm), (wg_st, wu_st),
                         (wg_b, wu_b), sems, l)

    @pl.when(b < nused)
    def _():
        x = x_ref[...]
        g = _dot(x, wg_b[...])
        u = _dot(x, wu_b[...])
        o_ref[...] = (g * _sigmoid(g) * u).astype(BF16)

    @pl.when(b >= nused)
    def _():
        o_ref[...] = jnp.zeros_like(o_ref)


def _expert_up(xs, w_gate, w_up, l, owner, blk_start, blk_end, nused):
    n_slots, d = xs.shape
    f = w_gate.shape[-1]
    tf = _pick(f, (512, 256, 128))
    n_blocks = n_slots // MOE_ROWS
    hbm = pl.BlockSpec(memory_space=pl.ANY)
    return pl.pallas_call(
        functools.partial(_expert_up_kernel, l=l),
        grid_spec=pltpu.PrefetchScalarGridSpec(
            num_scalar_prefetch=4,
            grid=(f // tf, n_blocks),
            in_specs=[pl.BlockSpec((MOE_ROWS, d), lambda j, b, *_: (b, 0)), hbm, hbm],
            out_specs=pl.BlockSpec((MOE_ROWS, tf), lambda j, b, *_: (b, j)),
            scratch_shapes=[pltpu.VMEM((d, tf), F32), pltpu.VMEM((d, tf), F32),
                            pltpu.VMEM((d, tf), BF16), pltpu.VMEM((d, tf), BF16),
                            pltpu.SemaphoreType.DMA((2,))],
        ),
        out_shape=jax.ShapeDtypeStruct((n_slots, f), BF16),
        compiler_params=_params("arbitrary", "arbitrary"),
        name="expert_up",
    )(owner, blk_start, blk_end, nused, xs, w_gate, w_up)


def _expert_down_kernel(owner_ref, start_ref, end_ref, nused_ref, x_ref, w_hbm, o_ref, w_st, w_b, sems, *, l):
    b = pl.program_id(1)
    nused = nused_ref[0]
    _expert_weights_step(owner_ref, start_ref, end_ref, nused, (w_hbm,), (w_st,), (w_b,), sems, l)

    @pl.when(b < nused)
    def _():
        o_ref[...] = _dot(x_ref[...], w_b[...])

    @pl.when(b >= nused)
    def _():
        o_ref[...] = jnp.zeros_like(o_ref)


def _expert_down(hmid, w_down, l, owner, blk_start, blk_end, nused):
    n_slots, f = hmid.shape
    d = w_down.shape[-1]
    tn = _pick(d, (1024, 512, 256, 128))
    n_blocks = n_slots // MOE_ROWS
    return pl.pallas_call(
        functools.partial(_expert_down_kernel, l=l),
        grid_spec=pltpu.PrefetchScalarGridSpec(
            num_scalar_prefetch=4,
            grid=(d // tn, n_blocks),
            in_specs=[pl.BlockSpec((MOE_ROWS, f), lambda j, b, *_: (b, 0)),
                      pl.BlockSpec(memory_space=pl.ANY)],
            out_specs=pl.BlockSpec((MOE_ROWS, tn), lambda j, b, *_: (b, j)),
            scratch_shapes=[pltpu.VMEM((f, tn), F32), pltpu.VMEM((f, tn), BF16),
                            pltpu.SemaphoreType.DMA((1,))],
        ),
        out_shape=jax.ShapeDtypeStruct((n_slots, d), F32),
        compiler_params=_params("arbitrary", "arbitrary"),
        name="expert_down",
    )(owner, blk_start, blk_end, nused, hmid, w_down)


def _combine_kernel(e0_ref, e1_ref, r0_ref, r1_ref, start_ref, h_ref, w_ref, g_ref, b_ref, y_ref,
                    o_ref, ob_ref, buf_ref, sem, *, alpha):
    i = pl.program_id(0)
    tb = h_ref.shape[0]

    def copies(r):
        t = i * tb + r
        s0 = start_ref[e0_ref[t]] * MOE_ROWS + r0_ref[t]
        s1 = start_ref[e1_ref[t]] * MOE_ROWS + r1_ref[t]
        return (pltpu.make_async_copy(y_ref.at[pl.ds(s0, 1), :], buf_ref.at[0, pl.ds(r, 1), :], sem),
                pltpu.make_async_copy(y_ref.at[pl.ds(s1, 1), :], buf_ref.at[1, pl.ds(r, 1), :], sem))

    def start(r, c):
        c0, c1 = copies(r)
        c0.start()
        c1.start()
        return c
    lax.fori_loop(0, tb, start, 0, unroll=8)

    def wait(r, c):
        c0, c1 = copies(r)
        c0.wait()
        c1.wait()
        return c
    lax.fori_loop(0, tb, wait, 0, unroll=8)

    w = w_ref[...]
    f = w[:, 0:1] * buf_ref[0] + w[:, 1:2] * buf_ref[1]
    out = _layer_norm_rows(alpha * h_ref[...] + f, g_ref[...], b_ref[...])
    o_ref[...] = out
    ob_ref[...] = out.astype(BF16)


def _combine(h, tok_w, g, b, y, e0, e1, r0, r1, blk_start, alpha):
    m, d = h.shape
    tb = 128
    row = lambda i, *_: (i, 0)
    vec = pl.BlockSpec((1, d), lambda i, *_: (0, 0))
    return pl.pallas_call(
        functools.partial(_combine_kernel, alpha=alpha),
        grid_spec=pltpu.PrefetchScalarGridSpec(
            num_scalar_prefetch=5,
            grid=(m // tb,),
            in_specs=[pl.BlockSpec((tb, d), row), pl.BlockSpec((tb, LANES), row), vec, vec,
                      pl.BlockSpec(memory_space=pl.ANY)],
            out_specs=[pl.BlockSpec((tb, d), row), pl.BlockSpec((tb, d), row)],
            scratch_shapes=[pltpu.VMEM((2, tb, d), F32), pltpu.SemaphoreType.DMA],
        ),
        out_shape=[jax.ShapeDtypeStruct((m, d), F32), jax.ShapeDtypeStruct((m, d), BF16)],
        compiler_params=_params("arbitrary"),
        name="moe_combine_ln",
    )(e0, e1, r0, r1, blk_start, h, tok_w, g.reshape(1, d), b.reshape(1, d), y)


def _sb_prompt_kernel(bias_ref, q_ref, k_ref, v_ref, o_ref, *, scale, nh):
    hblk = pl.program_id(1)
    i = pl.program_id(2)
    tq = q_ref.shape[0]
    row = lax.broadcasted_iota(I32, (tq, tq), 0)
    col = lax.broadcasted_iota(I32, (tq, tq), 1)
    later = (row > col).astype(BF16)
    below_diag = col < row

    def lanes(hh):
        return slice(hh * HEAD_DIM, (hh + 1) * HEAD_DIM)

    qs = [q_ref[:, lanes(hh)].astype(BF16) for hh in range(nh)]
    biases = [bias_ref[hblk * nh + hh] for hh in range(nh)]

    def key_block(hh, j, run, acc, valid):
        rows = pl.ds(pl.multiple_of(j * tq, tq), tq)
        k = k_ref[rows, lanes(hh)].astype(BF16)
        v = v_ref[rows, lanes(hh)].astype(BF16)
        z = _dot_nt(qs[hh], k) * scale + biases[hh]
        sp = jnp.maximum(z, 0.0) + jnp.log(1.0 + jnp.exp(-jnp.abs(z)))
        log_keep = -sp if valid is None else jnp.where(valid, -sp, 0.0)
        hi = log_keep.astype(BF16)
        lo = (log_keep - hi.astype(F32)).astype(BF16)
        after = _dot(hi, later) + _dot(lo, later)
        a = jnp.exp(z - sp + after + run)
        if valid is not None:
            a = jnp.where(valid, a, 0.0)
        acc = acc + _dot(a.astype(BF16), v)
        run = run + jnp.sum(log_keep, axis=-1, keepdims=True)
        return run, acc

    carry = []
    for hh in range(nh):
        carry.extend(key_block(hh, i, jnp.zeros((tq, 1), F32), jnp.zeros((tq, HEAD_DIM), F32), below_diag))

    def body(n, carry):
        out = []
        for hh in range(nh):
            out.extend(key_block(hh, i - n, carry[2 * hh], carry[2 * hh + 1], None))
        return tuple(out)

    carry = lax.fori_loop(1, i + 1, body, tuple(carry))
    for hh in range(nh):
        o_ref[:, lanes(hh)] = carry[2 * hh + 1].astype(o_ref.dtype)


def _sb_prompt(proj, bias, batch, seq, heads):
    tq = _pick(seq, (256, 128))
    nq = seq // tq
    nh = _pick(heads, (2, 1))
    width = nh * HEAD_DIM
    hblocks = heads // nh
    return pl.pallas_call(
        functools.partial(_sb_prompt_kernel, scale=HEAD_DIM ** -0.5, nh=nh),
        grid_spec=pltpu.PrefetchScalarGridSpec(
            num_scalar_prefetch=1,
            grid=(batch, hblocks, nq),
            in_specs=[pl.BlockSpec((tq, width), lambda b, h, i, *_: (b * nq + i, h)),
                      pl.BlockSpec((seq, width), lambda b, h, i, *_: (b, hblocks + h)),
                      pl.BlockSpec((seq, width), lambda b, h, i, *_: (b, 2 * hblocks + h))],
            out_specs=pl.BlockSpec((tq, width), lambda b, h, i, *_: (b * nq + i, h)),
        ),
        out_shape=jax.ShapeDtypeStruct((batch * seq, heads * HEAD_DIM), BF16),
        compiler_params=_params("arbitrary", "arbitrary", "arbitrary"),
        name="sb_prompt",
    )(bias, proj, proj, proj)


def _sb_sample_kernel(pt_ref, bias_ref, q_ref, kn_ref, vn_ref, *rest, scale, past_len, group):
    k_refs, v_refs = rest[:group], rest[group:2 * group]
    o_ref, run_ref, acc_ref = rest[2 * group:]
    p = pl.program_id(1)
    heads = q_ref.shape[0]
    page = k_refs[0].shape[0]
    width = page * heads
    q = q_ref[...]
    bias = bias_ref[...]

    @pl.when(p == 0)
    def _():
        z = jnp.sum(q * kn_ref[...], axis=-1, keepdims=True) * scale + bias
        valid = (lax.broadcasted_iota(I32, z.shape, 1) + past_len) < past_len
        sp = _softplus(z)
        run_ref[...] = jnp.where(valid, -sp, 0.0)
        acc_ref[...] = jnp.where(valid, jnp.exp(z - sp), 0.0) * vn_ref[...]

    qb = q.astype(BF16)
    lane = lax.broadcasted_iota(I32, (heads, width), 1)
    own = (lane % heads) == lax.broadcasted_iota(I32, (heads, width), 0)
    run = run_ref[...]
    acc = acc_ref[...]
    for g in range(group):
        k2 = k_refs[g][...].reshape(width, HEAD_DIM).astype(BF16)
        v2 = v_refs[g][...].reshape(width, HEAD_DIM).astype(BF16)
        z = _dot_nt(qb, k2) * scale + bias
        sp = jnp.maximum(z, 0.0) + jnp.log(1.0 + jnp.exp(-jnp.abs(z)))
        log_keep = jnp.where(own, -sp, 0.0)
        suffix = log_keep
        shift = heads
        while shift < width:
            moved = pltpu.roll(suffix, width - shift, axis=1)
            suffix = suffix + jnp.where(lane + shift < width, moved, 0.0)
            shift *= 2
        a = jnp.where(own, jnp.exp(z - sp + (suffix - log_keep) + run), 0.0)
        acc = acc + _dot(a.astype(BF16), v2)
        run = run + jnp.sum(log_keep, axis=-1, keepdims=True)
    run_ref[...] = run
    acc_ref[...] = acc

    @pl.when(p == pl.num_programs(1) - 1)
    def _():
        o_ref[...] = acc


def _sb_sample(q, k_new, v_new, cache_k, cache_v, page_table, bias, l):
    nseq, heads, _ = q.shape
    n_pages = page_table.shape[1]
    page = cache_k.shape[2]
    group = _pick(n_pages, (4, 2, 1))

    def page_map(g):
        return lambda b, p, pt: (l, pt[b * n_pages + (n_pages - 1 - (p * group + g))], 0, 0, 0)

    head_blk = pl.BlockSpec((None, heads, HEAD_DIM), lambda b, p, pt: (b, 0, 0))
    page_blks = [pl.BlockSpec((None, None, page, heads, HEAD_DIM), page_map(g)) for g in range(group)]
    return pl.pallas_call(
        functools.partial(_sb_sample_kernel, scale=HEAD_DIM ** -0.5, past_len=n_pages * page, group=group),
        grid_spec=pltpu.PrefetchScalarGridSpec(
            num_scalar_prefetch=1,
            grid=(nseq, n_pages // group),
            in_specs=[pl.BlockSpec((heads, 1), lambda b, p, pt: (0, 0)), head_blk, head_blk, head_blk]
                     + page_blks + page_blks,
            out_specs=head_blk,
            scratch_shapes=[pltpu.VMEM((heads, 1), F32), pltpu.VMEM((heads, HEAD_DIM), F32)],
        ),
        out_shape=jax.ShapeDtypeStruct((nseq, heads, HEAD_DIM), F32),
        compiler_params=_params("arbitrary", "arbitrary"),
        name="sb_sample",
    )(page_table.reshape(-1), bias.reshape(heads, 1), q, k_new, v_new,
      *([cache_k] * group), *([cache_v] * group))


def _hgrn_gates(hq, hf, log_lb, log_1m_lb, one_m_lb):
    a = log_lb
    b = log_1m_lb - _softplus(-hf)
    logf = jnp.maximum(a, b) + jnp.log1p(jnp.exp(-jnp.abs(a - b)))
    k = one_m_lb * _sigmoid(-hf)
    q = hq * _sigmoid(hq) * (HEAD_DIM ** -0.5)
    return q, k, logf


def _head_norm_gate(o, gain, gate):
    return o * lax.rsqrt(jnp.mean(o * o, axis=-1, keepdims=True) + NORM_EPS) * gain * gate


def _hgrn_prompt_kernel(hq_ref, hf_ref, hi_ref, hg_ref, llb_ref, l1m_ref, oml_ref, gain_ref,
                        o_ref, s_ref, cf_ref, st_ref):
    seq = hq_ref.shape[0]
    nh = st_ref.shape[0]
    c = HG_CHUNK
    blk = 128
    r = lax.broadcasted_iota(I32, (blk, blk), 0)
    s = lax.broadcasted_iota(I32, (blk, blk), 1)
    tri = ((r // c == s // c) & (s <= r)).astype(F32)

    def lanes(hh):
        return slice(hh * HEAD_DIM, (hh + 1) * HEAD_DIM)

    def prefix(n, carry):
        rows = pl.ds(pl.multiple_of(n * blk, blk), blk)
        for hh in range(nh):
            _, _, logf = _hgrn_gates(hq_ref[rows, lanes(hh)], hf_ref[rows, lanes(hh)],
                                     llb_ref[hh], l1m_ref[hh], oml_ref[hh])
            cf_ref[rows, lanes(hh)] = _dot_f32(tri, logf)
        return carry
    lax.fori_loop(0, seq // blk, prefix, 0)

    st_ref[...] = jnp.zeros_like(st_ref)
    t_idx = lax.broadcasted_iota(I32, (c, c, HEAD_DIM), 0)
    s_idx = lax.broadcasted_iota(I32, (c, c, HEAD_DIM), 1)
    causal = s_idx <= t_idx

    def step(n, carry):
        rows = pl.ds(pl.multiple_of(n * c, c), c)
        for hh in range(nh):
            q, k, _ = _hgrn_gates(hq_ref[rows, lanes(hh)], hf_ref[rows, lanes(hh)],
                                  llb_ref[hh], l1m_ref[hh], oml_ref[hh])
            v = hi_ref[rows, lanes(hh)]
            cf = cf_ref[rows, lanes(hh)]
            rel = jnp.where(causal, cf[:, None, :] - cf[None, :, :], 0.0)
            w = jnp.where(causal, jnp.exp(rel), 0.0)
            scores = jnp.sum(q[:, None, :] * k[None, :, :] * w, axis=-1, keepdims=True)
            o = jnp.sum(scores * v[None, :, :], axis=1)
            st = st_ref[hh]
            o = o + _dot_nt((q * jnp.exp(cf)).astype(BF16), st.astype(BF16))
            tot = cf[c - 1:c, :]
            kd = (k * jnp.exp(tot - cf)).astype(BF16)
            st_ref[hh] = jnp.exp(tot) * st + _dot_tn(v.astype(BF16), kd)
            g = hg_ref[rows, lanes(hh)]
            o_ref[rows, lanes(hh)] = _head_norm_gate(o, gain_ref[hh], g * _sigmoid(g)).astype(o_ref.dtype)
        return carry
    lax.fori_loop(0, seq // c, step, 0)
    for hh in range(nh):
        s_ref[hh] = st_ref[hh].T


def _hgrn_prompt(proj, col0, log_lb, log_1m_lb, one_m_lb, gain, batch, seq, heads):
    nh = _pick(heads, (4, 2, 1))
    width = nh * HEAD_DIM
    cb = col0 // width

    def col(which):
        return pl.BlockSpec((seq, width), lambda b, h: (b, cb + which * (heads // nh) + h))

    vec = pl.BlockSpec((nh, 1, HEAD_DIM), lambda b, h: (h, 0, 0))
    return pl.pallas_call(
        _hgrn_prompt_kernel,
        grid=(batch, heads // nh),
        in_specs=[col(0), col(1), col(2), col(3), vec, vec, vec, vec],
        out_specs=[pl.BlockSpec((seq, width), lambda b, h: (b, h)),
                   pl.BlockSpec((None, nh, HEAD_DIM, HEAD_DIM), lambda b, h: (b, h, 0, 0))],
        out_shape=[jax.ShapeDtypeStruct((batch * seq, heads * HEAD_DIM), BF16),
                   jax.ShapeDtypeStruct((batch, heads, HEAD_DIM, HEAD_DIM), F32)],
        scratch_shapes=[pltpu.VMEM((seq, width), F32), pltpu.VMEM((nh, HEAD_DIM, HEAD_DIM), F32)],
        compiler_params=_params("arbitrary", "arbitrary"),
        name="hgrn_prompt",
    )(proj, proj, proj, proj, log_lb, log_1m_lb, one_m_lb, gain)


def _to_column(row_vec):
    n = row_vec.shape[1]
    r = lax.broadcasted_iota(I32, (n, n), 0)
    c = lax.broadcasted_iota(I32, (n, n), 1)
    return jnp.sum(jnp.where(r == c, jnp.broadcast_to(row_vec, (n, n)), 0.0), axis=-1, keepdims=True)


def _hgrn_sample_kernel(hq_ref, hf_ref, hi_ref, hg_ref, llb_ref, l1m_ref, oml_ref, gain_ref, s0_ref,
                        o_ref, s_ref):
    q, k, logf = _hgrn_gates(hq_ref[...], hf_ref[...], llb_ref[...], l1m_ref[...], oml_ref[...])
    v = hi_ref[...]
    s_old = s0_ref[...]
    scores = jnp.sum(q * k, axis=-1, keepdims=True)
    qf_col = _to_column(_mxu_round(q * jnp.exp(logf)))
    o = scores * v + jnp.sum(qf_col * _mxu_round(s_old), axis=0, keepdims=True)
    s_ref[...] = _to_column(jnp.exp(logf)) * s_old + _to_column(k) * v
    g = hg_ref[...]
    o_ref[...] = _head_norm_gate(o, gain_ref[...], g * _sigmoid(g))


def _hgrn_sample(parts, log_lb, log_1m_lb, one_m_lb, gain, state, l):
    _, nseq, heads = parts.shape[:3]

    def part(which):
        return pl.BlockSpec((None, None, None, 1, HEAD_DIM), lambda b, h: (which, b, h, 0, 0))

    vec = pl.BlockSpec((None, 1, HEAD_DIM), lambda b, h: (h, 0, 0))
    return pl.pallas_call(
        _hgrn_sample_kernel,
        grid=(nseq, heads),
        in_specs=[part(0), part(1), part(2), part(3), vec, vec, vec, vec,
                  pl.BlockSpec((None, None, None, HEAD_DIM, HEAD_DIM), lambda b, h: (l, b, h, 0, 0))],
        out_specs=[pl.BlockSpec((None, None, 1, HEAD_DIM), lambda b, h: (b, h, 0, 0)),
                   pl.BlockSpec((None, None, HEAD_DIM, HEAD_DIM), lambda b, h: (b, h, 0, 0))],
        out_shape=[jax.ShapeDtypeStruct((nseq, heads, 1, HEAD_DIM), F32),
                   jax.ShapeDtypeStruct((nseq, heads, HEAD_DIM, HEAD_DIM), F32)],
        compiler_params=_params("arbitrary", "arbitrary"),
        name="hgrn_sample",
    )(parts, parts, parts, parts, log_lb, log_1m_lb, one_m_lb, gain, state)


def _mlstm_prompt_kernel(bias_ref, q_ref, k_ref, v_ref, og_ref, gcol_ref, grow_ref, gain_ref,
                         o_ref, c_ref, n_ref, m_ref, cst_ref, nst_ref, mst_ref):
    h = pl.program_id(1)
    heads = pl.num_programs(1)
    seq = q_ref.shape[0]
    c = min(ML_CHUNK, seq)
    b_i = bias_ref[h]
    b_f = bias_ref[heads + h]
    cst_ref[...] = jnp.zeros_like(cst_ref)
    nst_ref[...] = jnp.zeros_like(nst_ref)
    mst_ref[...] = jnp.zeros_like(mst_ref)
    t_idx = lax.broadcasted_iota(I32, (c, c), 0)
    s_idx = lax.broadcasted_iota(I32, (c, c), 1)
    causal = s_idx <= t_idx
    gain = gain_ref[...]

    def step(n, carry):
        off = pl.multiple_of(n * c, c)
        rows = pl.ds(off, c)
        li_col = gcol_ref[0, rows, :] + b_i
        lf_col = -_softplus(-(gcol_ref[1, rows, :] + b_f))
        li_row = grow_ref[0, :, rows] + b_i
        lf_row = -_softplus(-(grow_ref[1, :, rows] + b_f))
        lf_rows = jnp.broadcast_to(lf_row, (c, c))
        lf_cols = jnp.broadcast_to(lf_col, (c, c))
        cf_col = jnp.sum(jnp.where(causal, lf_rows, 0.0), axis=-1, keepdims=True)
        cf_row = jnp.sum(jnp.where(t_idx <= s_idx, lf_cols, 0.0), axis=0, keepdims=True)
        m_old = mst_ref[...]
        logd = jnp.where(causal, cf_col - cf_row + li_row, NEG_INF)
        a_col = cf_col + m_old
        mt = jnp.maximum(a_col, jnp.max(logd, axis=-1, keepdims=True))
        q = q_ref[rows, :].astype(BF16)
        k = (k_ref[rows, :] * (HEAD_DIM ** -0.5))
        v = v_ref[rows, :].astype(BF16)
        w = jnp.where(causal, jnp.exp(logd - mt), 0.0) * _dot_nt(q, k.astype(BF16))
        wi = jnp.exp(a_col - mt)
        cst = cst_ref[...]
        nst = nst_ref[...]
        num = _dot(w.astype(BF16), v) + wi * _dot(q, cst.astype(BF16))
        den = jnp.sum(w, axis=-1, keepdims=True) + wi * jnp.sum(q_ref[rows, :] * nst, axis=-1, keepdims=True)
        hid = num / jnp.maximum(jnp.abs(den), jnp.exp(-mt))
        tot = jnp.sum(lf_row, axis=-1, keepdims=True)
        g_row = tot - cf_row + li_row
        g_col = tot - cf_col + li_col
        m_new = jnp.maximum(tot + m_old, jnp.max(g_row, axis=-1, keepdims=True))
        decay = jnp.exp(tot + m_old - m_new)
        wk = jnp.exp(g_col - m_new) * k
        cst_ref[...] = decay * cst + _dot_tn(wk.astype(BF16), v)
        nst_ref[...] = decay * nst + jnp.sum(wk, axis=0, keepdims=True)
        mst_ref[...] = m_new
        o_ref[rows, :] = _head_norm_gate(hid, gain, _sigmoid(og_ref[rows, :])).astype(o_ref.dtype)
        return carry
    lax.fori_loop(0, seq // c, step, 0)
    c_ref[...] = cst_ref[...]
    n_ref[...] = nst_ref[...]
    m_ref[...] = mst_ref[...]


def _mlstm_prompt(proj, col0, gate_cols, gate_rows, bias, gain, batch, seq, heads):
    cb = col0 // HEAD_DIM

    def col(which):
        return pl.BlockSpec((seq, HEAD_DIM), lambda b, h, *_: (b, cb + which * heads + h))

    return pl.pallas_call(
        _mlstm_prompt_kernel,
        grid_spec=pltpu.PrefetchScalarGridSpec(
            num_scalar_prefetch=1,
            grid=(batch, heads),
            in_specs=[col(0), col(1), col(2), col(3),
                      pl.BlockSpec((None, None, 2, seq, 1), lambda b, h, *_: (b, h, 0, 0, 0)),
                      pl.BlockSpec((None, None, 2, 1, seq), lambda b, h, *_: (b, h, 0, 0, 0)),
                      pl.BlockSpec((None, 1, HEAD_DIM), lambda b, h, *_: (h, 0, 0))],
            out_specs=[pl.BlockSpec((seq, HEAD_DIM), lambda b, h, *_: (b, h)),
                       pl.BlockSpec((None, None, HEAD_DIM, HEAD_DIM), lambda b, h, *_: (b, h, 0, 0)),
                       pl.BlockSpec((None, None, 1, HEAD_DIM), lambda b, h, *_: (b, h, 0, 0)),
                       pl.BlockSpec((None, None, 1, 1), lambda b, h, *_: (b, h, 0, 0))],
            scratch_shapes=[pltpu.VMEM((HEAD_DIM, HEAD_DIM), F32), pltpu.VMEM((1, HEAD_DIM), F32),
                            pltpu.VMEM((1, 1), F32)],
        ),
        out_shape=[jax.ShapeDtypeStruct((batch * seq, heads * HEAD_DIM), BF16),
                   jax.ShapeDtypeStruct((batch, heads, HEAD_DIM, HEAD_DIM), F32),
                   jax.ShapeDtypeStruct((batch, heads, 1, HEAD_DIM), F32),
                   jax.ShapeDtypeStruct((batch, heads, 1, 1), F32)],
        compiler_params=_params("arbitrary", "arbitrary"),
        name="mlstm_prompt",
    )(bias, proj, proj, proj, proj, gate_cols, gate_rows, gain)


def _mlstm_sample_kernel(bias_ref, q_ref, k_ref, v_ref, og_ref, gi_ref, gf_ref, gain_ref,
                         c0_ref, n0_ref, m0_ref, o_ref, c_ref, n_ref, m_ref):
    h = pl.program_id(1)
    heads = pl.num_programs(1)
    li = gi_ref[...] + bias_ref[h]
    lf = -_softplus(-(gf_ref[...] + bias_ref[heads + h]))
    q = q_ref[...]
    k = k_ref[...] * (HEAD_DIM ** -0.5)
    v = v_ref[...]
    c_old, n_old, m_old = c0_ref[...], n0_ref[...], m0_ref[...]
    a = lf + m_old
    mt = jnp.maximum(a, li)
    w = jnp.exp(li - mt) * jnp.sum(q * k, axis=-1, keepdims=True)
    wi = jnp.exp(a - mt)
    num = w * v + wi * jnp.sum(_to_column(_mxu_round(q)) * _mxu_round(c_old), axis=0, keepdims=True)
    den = w + wi * jnp.sum(q * n_old, axis=-1, keepdims=True)
    hid = num / jnp.maximum(jnp.abs(den), jnp.exp(-mt))
    m_new = jnp.maximum(lf + m_old, li)
    decay = jnp.exp(lf + m_old - m_new)
    ws = jnp.exp(li - m_new)
    c_ref[...] = decay * c_old + _to_column(ws * k) * v
    n_ref[...] = decay * n_old + ws * k
    m_ref[...] = m_new
    o_ref[...] = _head_norm_gate(hid, gain_ref[...], _sigmoid(og_ref[...]))


def _mlstm_sample(parts, gate_i, gate_f, bias, gain, c0, n0, m0, l):
    _, nseq, heads = parts.shape[:3]

    def part(which):
        return pl.BlockSpec((None, None, None, 1, HEAD_DIM), lambda b, h, *_: (which, b, h, 0, 0))

    one = pl.BlockSpec((None, None, 1, 1), lambda b, h, *_: (b, h, 0, 0))
    vec_o = pl.BlockSpec((None, None, 1, HEAD_DIM), lambda b, h, *_: (b, h, 0, 0))
    mat_o = pl.BlockSpec((None, None, HEAD_DIM, HEAD_DIM), lambda b, h, *_: (b, h, 0, 0))
    return pl.pallas_call(
        _mlstm_sample_kernel,
        grid_spec=pltpu.PrefetchScalarGridSpec(
            num_scalar_prefetch=1,
            grid=(nseq, heads),
            in_specs=[part(0), part(1), part(2), part(3), one, one,
                      pl.BlockSpec((None, 1, HEAD_DIM), lambda b, h, *_: (h, 0, 0)),
                      pl.BlockSpec((None, None, None, HEAD_DIM, HEAD_DIM), lambda b, h, *_: (l, b, h, 0, 0)),
                      pl.BlockSpec((None, None, None, 1, HEAD_DIM), lambda b, h, *_: (l, b, h, 0, 0)),
                      pl.BlockSpec((None, None, None, 1, 1), lambda b, h, *_: (l, b, h, 0, 0))],
            out_specs=[vec_o, mat_o, vec_o, one],
        ),
        out_shape=[jax.ShapeDtypeStruct((nseq, heads, 1, HEAD_DIM), F32),
                   jax.ShapeDtypeStruct((nseq, heads, HEAD_DIM, HEAD_DIM), F32),
                   jax.ShapeDtypeStruct((nseq, heads, 1, HEAD_DIM), F32),
                   jax.ShapeDtypeStruct((nseq, heads, 1, 1), F32)],
        compiler_params=_params("arbitrary", "arbitrary"),
        name="mlstm_sample",
    )(bias, parts, parts, parts, parts, gate_i, gate_f, gain, c0, n0, m0)


def _with_tail(prompt_rows, sample_rows):
    ns = sample_rows.shape[0]
    pad = jnp.zeros((TAIL_ROWS - ns, sample_rows.shape[1]), sample_rows.dtype)
    return jnp.concatenate([prompt_rows, sample_rows, pad], axis=0)


def kernel(x_prompt, x_sample, cache_k, cache_v, page_table, state_hgrn, state_mlstm_c, state_mlstm_n,
           state_mlstm_m, w_in, sb_bias, hg_lb_logits, hg_norm, ml_bias_i, ml_bias_f, ml_norm, w_br_sb,
           w_br_hg, w_br_ml, w_out, ln1_g, ln1_b, w_router_group, b_router_group, w_router_expert,
           b_router_expert, w_e_gate, w_e_up, w_e_down, ln2_g, ln2_b):
    batch, seq, d_model = x_prompt.shape
    nseq = x_sample.shape[0]
    depth = w_in.shape[0]
    n_prompt = batch * seq
    sb_heads = sb_bias.shape[1]
    sb_w = sb_heads * HEAD_DIM
    hg_w = hg_norm.shape[1]
    hg_heads = hg_w // HEAD_DIM
    ml_w = ml_norm.shape[1]
    ml_heads = ml_w // HEAD_DIM
    col_hg = 3 * sb_w
    col_ml = col_hg + 4 * hg_w
    col_if = col_ml + 4 * ml_w
    col_gate = col_if + 2 * ml_heads
    assert x_sample.shape[1] == 1 and nseq <= TAIL_ROWS and w_in.shape[2] == col_gate + 3 * d_model
    alpha = (2 * depth) ** 0.25
    n_tok = n_prompt + TAIL_ROWS
    n_blocks = -(-2 * n_tok // MOE_ROWS) + N_EXPERTS
    assert n_blocks <= LANES

    w_in_t = jnp.swapaxes(w_in, 1, 2)
    lb_all = jnp.cumsum(jax.nn.softmax(hg_lb_logits.astype(F32), axis=0), axis=0)
    lower = lb_all - lb_all[0]

    x = _with_tail(x_prompt.reshape(n_prompt, d_model), x_sample.reshape(nseq, d_model))
    xb = x.astype(BF16)
    outs = {name: [] for name in ("kp", "vp", "ks", "vs", "hp", "hs", "cp", "np", "mp", "cs", "ns", "ms")}

    for l in range(depth):
        proj = _matmul_t(xb, w_in_t, l, row0=0, nrows=col_if)
        gate_if = _matmul_t(xb, w_in_t, l, row0=col_if, nrows=LANES)
        gates = _matmul_t(xb, w_in_t, l, row0=col_gate, nrows=3 * d_model, act="sigmoid")

        tail = proj[n_prompt:n_prompt + nseq]
        outs["kp"].append(proj[:n_prompt, sb_w:2 * sb_w].reshape(batch, seq, sb_heads, HEAD_DIM))
        outs["vp"].append(proj[:n_prompt, 2 * sb_w:3 * sb_w].reshape(batch, seq, sb_heads, HEAD_DIM))
        q_s = tail[:, :sb_w].reshape(nseq, sb_heads, HEAD_DIM)
        k_s = tail[:, sb_w:2 * sb_w].reshape(nseq, sb_heads, HEAD_DIM)
        v_s = tail[:, 2 * sb_w:3 * sb_w].reshape(nseq, sb_heads, HEAD_DIM)
        outs["ks"].append(k_s.reshape(nseq, 1, sb_heads, HEAD_DIM))
        outs["vs"].append(v_s.reshape(nseq, 1, sb_heads, HEAD_DIM))

        o_sb_p = _sb_prompt(proj, sb_bias[l], batch, seq, sb_heads)
        o_sb_s = _sb_sample(q_s, k_s, v_s, cache_k, cache_v, page_table, sb_bias[l], l)
        o_sb = _with_tail(o_sb_p, o_sb_s.reshape(nseq, sb_w).astype(BF16))

        lb = lower[l].reshape(hg_heads, 1, HEAD_DIM)
        log_lb, log_1m_lb, one_m_lb = jnp.log(lb), jnp.log1p(-lb), 1.0 - lb
        hg_gain = hg_norm[l].reshape(hg_heads, 1, HEAD_DIM)
        o_hg_p, s_p = _hgrn_prompt(proj, col_hg, log_lb, log_1m_lb, one_m_lb, hg_gain, batch, seq, hg_heads)
        hg_parts = tail[:, col_hg:col_ml].reshape(nseq, 4, hg_heads, 1, HEAD_DIM).swapaxes(0, 1)
        o_hg_s, s_s = _hgrn_sample(hg_parts, log_lb, log_1m_lb, one_m_lb, hg_gain, state_hgrn, l)
        o_hg = _with_tail(o_hg_p, o_hg_s.reshape(nseq, hg_w).astype(BF16))
        outs["hp"].append(s_p)
        outs["hs"].append(s_s)

        ml_bias = jnp.concatenate([ml_bias_i[l], ml_bias_f[l]]).astype(F32)
        ml_gain = ml_norm[l].reshape(ml_heads, 1, HEAD_DIM)
        g_p = gate_if[:n_prompt, :2 * ml_heads].reshape(batch, seq, 2, ml_heads).transpose(0, 3, 2, 1)
        o_ml_p, c_p, n_p, m_p = _mlstm_prompt(proj, col_ml, g_p[..., None], g_p[:, :, :, None, :], ml_bias,
                                              ml_gain, batch, seq, ml_heads)
        ml_parts = tail[:, col_ml:col_if].reshape(nseq, 4, ml_heads, 1, HEAD_DIM).swapaxes(0, 1)
        g_s = gate_if[n_prompt:n_prompt + nseq, :2 * ml_heads].reshape(nseq, 2, ml_heads, 1, 1)
        o_ml_s, c_s, n_s, m_s = _mlstm_sample(
            ml_parts, g_s[:, 0], g_s[:, 1], ml_bias, ml_gain, state_mlstm_c,
            state_mlstm_n.reshape(depth, nseq, ml_heads, 1, HEAD_DIM),
            state_mlstm_m.reshape(depth, nseq, ml_heads, 1, 1), l)
        o_ml = _with_tail(o_ml_p, o_ml_s.reshape(nseq, ml_w).astype(BF16))
        outs["cp"].append(c_p)
        outs["np"].append(n_p.reshape(batch, ml_heads, HEAD_DIM))
        outs["mp"].append(m_p.reshape(batch, ml_heads))
        outs["cs"].append(c_s)
        outs["ns"].append(n_s.reshape(nseq, ml_heads, HEAD_DIM))
        outs["ms"].append(m_s.reshape(nseq, ml_heads))

        merged = _merge(o_sb, o_hg, o_ml, gates, w_br_sb, w_br_hg, w_br_ml, l)
        mix = _matmul(merged, w_out, lead=(l,))
        w_r = jnp.pad(jnp.concatenate([w_router_group[l], w_router_expert[l]], axis=1).astype(F32),
                      ((0, 0), (0, LANES - N_GROUPS - N_EXPERTS)))
        b_r = jnp.pad(jnp.concatenate([b_router_group[l], b_router_expert[l]]).astype(F32),
                      (0, LANES - N_GROUPS - N_EXPERTS)).reshape(1, LANES)
        h, logits = _ln_router(x, mix, ln1_g[l], ln1_b[l], w_r, alpha)

        tok_i, tok_w, counts = _route(logits, b_r)
        info = _layout(counts)
        blk_start = info[0, :N_EXPERTS]
        nused = info[8, N_EXPERTS - 1:N_EXPERTS]
        owner = info[16:16 + n_blocks, 0]
        e0, e1, r0, r1 = tok_i[:, 0], tok_i[:, 1], tok_i[:, 2], tok_i[:, 3]
        xs = _gather_slots(h, e0, e1, r0, r1, blk_start, counts[0, :N_EXPERTS], nused, n_blocks)
        blk_end = info[8, :N_EXPERTS]
        hmid = _expert_up(xs, w_e_gate, w_e_up, l, owner, blk_start, blk_end, nused)
        y = _expert_down(hmid, w_e_down, l, owner, blk_start, blk_end, nused)
        x, xb = _combine(h, tok_w, ln2_g[l], ln2_b[l], y, e0, e1, r0, r1, blk_start, alpha)

    st = {k: jnp.stack(v) for k, v in outs.items()}
    y_prompt = x[:n_prompt].reshape(batch, seq, d_model)
    y_sample = x[n_prompt:n_prompt + nseq].reshape(nseq, 1, d_model)
    return (y_prompt, y_sample, st["kp"], st["vp"], st["ks"], st["vs"], st["hp"], st["hs"],
            st["cp"], st["np"], st["mp"], st["cs"], st["ns"], st["ms"])
```

```python
import functools

import jax
import jax.numpy as jnp
from jax import lax
from jax.experimental import pallas as pl
from jax.experimental.pallas import tpu as pltpu

F32 = jnp.float32
BF16 = jnp.bfloat16
I32 = jnp.int32

LANES = 128
HEAD_DIM = 128
TAIL_ROWS = 128
VMEM_LIMIT = 56 * 1024 * 1024
LN_EPS = 1e-5
NORM_EPS = 1e-6
N_GROUPS = 4
EXPERTS_PER_GROUP = 8
N_EXPERTS = N_GROUPS * EXPERTS_PER_GROUP
MOE_ROWS = 256
HG_CHUNK = 16
ML_CHUNK = 128
NEG_INF = float("-inf")


def _params(*sem):
    return pltpu.CompilerParams(dimension_semantics=sem, vmem_limit_bytes=VMEM_LIMIT)


def _pick(n, candidates):
    for c in candidates:
        if n % c == 0:
            return c
    return n


def _softplus(x):
    return jnp.maximum(x, 0.0) + jnp.log1p(jnp.exp(-jnp.abs(x)))


def _sigmoid(x):
    return 1.0 / (1.0 + jnp.exp(-x))


def _dot(a, b):
    return jnp.dot(a, b, preferred_element_type=F32)


def _dot_nt(a, b):
    return lax.dot_general(a, b, (((1,), (1,)), ((), ())), preferred_element_type=F32)


def _dot_tn(a, b):
    return lax.dot_general(a, b, (((0,), (0,)), ((), ())), preferred_element_type=F32)


def _mxu_round(x):
    return x.astype(BF16).astype(F32)


def _dot_f32(a, b):
    return jnp.dot(a, b, preferred_element_type=F32, precision=lax.Precision.HIGHEST)


def _mm_kernel(x_ref, w_ref, o_ref, wb_ref, *, act):
    @pl.when(pl.program_id(1) == 0)
    def _():
        wb_ref[...] = w_ref[...].astype(BF16)

    acc = _dot(x_ref[...], wb_ref[...])
    if act == "sigmoid":
        acc = _sigmoid(acc)
    o_ref[...] = acc.astype(o_ref.dtype)


def _matmul(x, w, *, lead=(), col0=0, ncols=None, out_dtype=F32, act=None, tn=512):
    m, k = x.shape
    ncols = w.shape[-1] - col0 if ncols is None else ncols
    tn = _pick(ncols, (tn, 256, 128))
    tm = _pick(m, (640, 512, 256, 128))
    assert col0 % tn == 0 and w.shape[-2] == k
    nlead = len(lead)
    w_spec = pl.BlockSpec((None,) * nlead + (k, tn), lambda j, i: tuple(lead) + (0, col0 // tn + j))
    return pl.pallas_call(
        functools.partial(_mm_kernel, act=act),
        grid=(ncols // tn, m // tm),
        in_specs=[pl.BlockSpec((tm, k), lambda j, i: (i, 0)), w_spec],
        out_specs=pl.BlockSpec((tm, tn), lambda j, i: (i, j)),
        out_shape=jax.ShapeDtypeStruct((m, ncols), out_dtype),
        scratch_shapes=[pltpu.VMEM((k, tn), BF16)],
        compiler_params=_params("arbitrary", "arbitrary"),
        name="dense_matmul",
    )(x, w)


def _mm_t_kernel(x_ref, wt_ref, o_ref, wb_ref, *, act):
    @pl.when(pl.program_id(1) == 0)
    def _():
        wb_ref[...] = wt_ref[0].T.astype(BF16)

    acc = _dot(x_ref[...], wb_ref[...])
    if act == "sigmoid":
        acc = _sigmoid(acc)
    o_ref[...] = acc.astype(o_ref.dtype)


def _matmul_t(x, wt, l, *, row0, nrows, out_dtype=F32, act=None):
    m, k = x.shape
    tn = _pick(nrows, (512, 256, 128))
    tm = _pick(m, (640, 512, 256, 128))
    assert wt.shape[2] == k and row0 % 8 == 0
    if row0 % tn == 0:
        w_spec = pl.BlockSpec((1, tn, k), lambda j, i: (l, row0 // tn + j, 0))
    else:
        w_spec = pl.BlockSpec((pl.Element(1), pl.Element(tn), pl.Element(k)),
                              lambda j, i: (l, pl.multiple_of(row0 + j * tn, 8), 0))
    return pl.pallas_call(
        functools.partial(_mm_t_kernel, act=act),
        grid=(nrows // tn, m // tm),
        in_specs=[pl.BlockSpec((tm, k), lambda j, i: (i, 0)), w_spec],
        out_specs=pl.BlockSpec((tm, tn), lambda j, i: (i, j)),
        out_shape=jax.ShapeDtypeStruct((m, nrows), out_dtype),
        scratch_shapes=[pltpu.VMEM((k, tn), BF16)],
        compiler_params=_params("arbitrary", "arbitrary"),
        name="dense_matmul_t",
    )(x, wt)


def _merge_kernel(osb_ref, ohg_ref, oml_ref, g_sb_ref, g_hg_ref, g_ml_ref, wsb_ref, whg_ref, wml_ref,
                  o_ref, wsb_b, whg_b, wml_b):
    @pl.when(pl.program_id(1) == 0)
    def _():
        wsb_b[...] = wsb_ref[...].astype(BF16)
        whg_b[...] = whg_ref[...].astype(BF16)
        wml_b[...] = wml_ref[...].astype(BF16)

    merged = (g_sb_ref[...] * _dot(osb_ref[...], wsb_b[...])
              + g_hg_ref[...] * _dot(ohg_ref[...], whg_b[...])
              + g_ml_ref[...] * _dot(oml_ref[...], wml_b[...]))
    o_ref[...] = merged.astype(o_ref.dtype)


def _merge(o_sb, o_hg, o_ml, gates, w_sb, w_hg, w_ml, l):
    m = o_sb.shape[0]
    d = w_sb.shape[-1]
    tn = _pick(d, (512, 256, 128))
    tm = _pick(m, (640, 512, 256, 128))
    nj = d // tn

    def lhs(a):
        return pl.BlockSpec((tm, a.shape[1]), lambda j, i: (i, 0))

    def wspec(w):
        return pl.BlockSpec((None, w.shape[1], tn), lambda j, i: (l, 0, j))

    def gspec(which):
        return pl.BlockSpec((tm, tn), lambda j, i: (i, which * nj + j))

    return pl.pallas_call(
        _merge_kernel,
        grid=(nj, m // tm),
        in_specs=[lhs(o_sb), lhs(o_hg), lhs(o_ml), gspec(0), gspec(1), gspec(2),
                  wspec(w_sb), wspec(w_hg), wspec(w_ml)],
        out_specs=pl.BlockSpec((tm, tn), lambda j, i: (i, j)),
        out_shape=jax.ShapeDtypeStruct((m, d), BF16),
        scratch_shapes=[pltpu.VMEM((w_sb.shape[1], tn), BF16), pltpu.VMEM((w_hg.shape[1], tn), BF16),
                        pltpu.VMEM((w_ml.shape[1], tn), BF16)],
        compiler_params=_params("arbitrary", "arbitrary"),
        name="branch_merge",
    )(o_sb, o_hg, o_ml, gates, gates, gates, w_sb, w_hg, w_ml)


def _layer_norm_rows(z, g, b):
    mu = jnp.mean(z, axis=-1, keepdims=True)
    zc = z - mu
    var = jnp.mean(zc * zc, axis=-1, keepdims=True)
    return zc * lax.rsqrt(var + LN_EPS) * g + b


def _ln_router_kernel(x_ref, y_ref, g_ref, b_ref, wr_ref, h_ref, lg_ref, *, alpha):
    h = _layer_norm_rows(alpha * x_ref[...] + y_ref[...], g_ref[...], b_ref[...])
    h_ref[...] = h
    lg_ref[...] = _dot(h.astype(BF16), wr_ref[...].astype(BF16))


def _ln_router(x, y, g, b, w_r, alpha):
    m, d = x.shape
    tm = _pick(m, (320, 256, 128))
    row = pl.BlockSpec((tm, d), lambda i: (i, 0))
    vec = pl.BlockSpec((1, d), lambda i: (0, 0))
    return pl.pallas_call(
        functools.partial(_ln_router_kernel, alpha=alpha),
        grid=(m // tm,),
        in_specs=[row, row, vec, vec, pl.BlockSpec((d, LANES), lambda i: (0, 0))],
        out_specs=[row, pl.BlockSpec((tm, LANES), lambda i: (i, 0))],
        out_shape=[jax.ShapeDtypeStruct((m, d), F32), jax.ShapeDtypeStruct((m, LANES), F32)],
        compiler_params=_params("arbitrary"),
        name="ln_router",
    )(x, y, g.reshape(1, d), b.reshape(1, d), w_r)


def _first_index_of_max(vals, lane):
    mx = jnp.max(vals, axis=-1, keepdims=True)
    idx = jnp.min(jnp.where(vals == mx, lane.astype(F32), float(LANES)), axis=-1, keepdims=True)
    return mx, idx.astype(I32)


def _route_kernel(lg_ref, b_ref, tok_i_ref, tok_w_ref, cnt_ref, run_ref):
    i = pl.program_id(0)

    @pl.when(i == 0)
    def _():
        run_ref[...] = jnp.zeros_like(run_ref)

    tb = lg_ref.shape[0]
    logits = lg_ref[...] + b_ref[...]
    lane = lax.broadcasted_iota(I32, (tb, LANES), 1)
    is_group = lane < N_GROUPS
    gl = jnp.where(is_group, logits, NEG_INF)
    gmax, g_star = _first_index_of_max(gl, lane)
    p_g = 1.0 / jnp.sum(jnp.where(is_group, jnp.exp(gl - gmax), 0.0), axis=-1, keepdims=True)

    e_lane = lane - N_GROUPS
    in_group = (e_lane >= g_star * EXPERTS_PER_GROUP) & (e_lane < (g_star + 1) * EXPERTS_PER_GROUP)
    v1, l1 = _first_index_of_max(jnp.where(in_group, logits, NEG_INF), lane)
    v2, l2 = _first_index_of_max(jnp.where(in_group & (lane != l1), logits, NEG_INF), lane)
    ex = jnp.exp(v2 - v1)
    w0 = p_g / (1.0 + ex)
    w1 = p_g * ex / (1.0 + ex)
    e0 = l1 - N_GROUPS
    e1 = l2 - N_GROUPS

    onehot = ((lane == e0) | (lane == e1)).astype(F32)
    row = lax.broadcasted_iota(I32, (tb, tb), 0)
    col = lax.broadcasted_iota(I32, (tb, tb), 1)
    before = (col < row).astype(BF16)
    prior = _dot(before, onehot.astype(BF16)) + run_ref[...]
    r0 = jnp.sum(jnp.where(lane == e0, prior, 0.0), axis=-1, keepdims=True).astype(I32)
    r1 = jnp.sum(jnp.where(lane == e1, prior, 0.0), axis=-1, keepdims=True).astype(I32)
    run_ref[...] = run_ref[...] + jnp.sum(onehot, axis=0, keepdims=True)

    tok_i_ref[...] = jnp.where(lane == 0, e0, jnp.where(lane == 1, e1, jnp.where(lane == 2, r0, r1)))
    tok_w_ref[...] = jnp.where(lane == 0, w0, w1)

    @pl.when(i == pl.num_programs(0) - 1)
    def _():
        cnt_ref[...] = jnp.broadcast_to(run_ref[...], cnt_ref.shape).astype(I32)


def _route(logits, bias_row):
    m = logits.shape[0]
    tb = _pick(m, (640, 512, 256, 128))
    blk = pl.BlockSpec((tb, LANES), lambda i: (i, 0))
    return pl.pallas_call(
        _route_kernel,
        grid=(m // tb,),
        in_specs=[blk, pl.BlockSpec((1, LANES), lambda i: (0, 0))],
        out_specs=[blk, blk, pl.BlockSpec((8, LANES), lambda i: (0, 0))],
        out_shape=[jax.ShapeDtypeStruct((m, LANES), I32), jax.ShapeDtypeStruct((m, LANES), F32),
                   jax.ShapeDtypeStruct((8, LANES), I32)],
        scratch_shapes=[pltpu.VMEM((1, LANES), F32)],
        compiler_params=_params("arbitrary"),
        name="route",
    )(logits, bias_row)


def _layout_kernel(cnt_ref, info_ref):
    cnt = cnt_ref[...].astype(F32)
    lane = lax.broadcasted_iota(I32, cnt.shape, 1)
    nblk = jnp.where(lane < N_EXPERTS, jnp.floor((cnt + (MOE_ROWS - 1)) * (1.0 / MOE_ROWS)), 0.0)
    src = lax.broadcasted_iota(I32, (LANES, LANES), 0)
    dst = lax.broadcasted_iota(I32, (LANES, LANES), 1)
    blk_end = _dot_f32(nblk, (src <= dst).astype(F32))
    blk_start = blk_end - nblk
    b_idx = lax.broadcasted_iota(I32, (LANES, LANES), 0).astype(F32)
    ended = (jnp.broadcast_to(blk_end[0:1, :], (LANES, LANES)) <= b_idx) & (dst < N_EXPERTS)
    owner = jnp.minimum(jnp.sum(ended.astype(F32), axis=-1, keepdims=True), float(N_EXPERTS - 1))
    info_ref[0:8, :] = blk_start.astype(I32)
    info_ref[8:16, :] = blk_end.astype(I32)
    info_ref[16:16 + LANES, :] = jnp.broadcast_to(owner, (LANES, LANES)).astype(I32)


def _layout(counts):
    return pl.pallas_call(
        _layout_kernel,
        out_shape=jax.ShapeDtypeStruct((16 + LANES, LANES), I32),
        name="moe_layout",
    )(counts)


def _gather_kernel(e0_ref, e1_ref, r0_ref, r1_ref, start_ref, cnt_ref, nused_ref, h_ref, o_ref,
                   slot_ref, buf_ref, sem):
    b = pl.program_id(0)
    n_tok = e0_ref.shape[0]
    rows = buf_ref.shape[1]
    nused = nused_ref[0]

    def copy(blk, r):
        slot = blk % 2
        return pltpu.make_async_copy(h_ref.at[pl.ds(slot_ref[blk * rows + r], 1), :],
                                     buf_ref.at[slot, pl.ds(r, 1), :], sem.at[slot])

    def issue(blk):
        def start(r, c):
            copy(blk, r).start()
            return c
        lax.fori_loop(0, rows, start, 0, unroll=8)

    @pl.when(b == 0)
    def _():
        def pad_expert(e, c):
            base = start_ref[e] * rows
            used = cnt_ref[e]

            def pad(s, c2):
                slot_ref[base + s] = 0
                return c2
            return lax.fori_loop(used, (used + rows - 1) // rows * rows, pad, c)
        lax.fori_loop(0, N_EXPERTS, pad_expert, 0)

        def place(t, c):
            slot_ref[start_ref[e0_ref[t]] * rows + r0_ref[t]] = t
            slot_ref[start_ref[e1_ref[t]] * rows + r1_ref[t]] = t
            return c
        lax.fori_loop(0, n_tok, place, 0)

        @pl.when(nused > 0)
        def _():
            issue(0)

    @pl.when(b + 1 < nused)
    def _():
        issue(b + 1)

    @pl.when(b < nused)
    def _():
        def wait(r, c):
            copy(b, r).wait()
            return c
        lax.fori_loop(0, rows, wait, 0, unroll=8)
        o_ref[...] = buf_ref[b % 2].astype(BF16)

    @pl.when(b >= nused)
    def _():
        o_ref[...] = jnp.zeros_like(o_ref)


def _gather_slots(h, e0, e1, r0, r1, blk_start, counts, nused, n_blocks):
    n_tok, d = h.shape
    return pl.pallas_call(
        _gather_kernel,
        grid_spec=pltpu.PrefetchScalarGridSpec(
            num_scalar_prefetch=7,
            grid=(n_blocks,),
            in_specs=[pl.BlockSpec(memory_space=pl.ANY)],
            out_specs=pl.BlockSpec((MOE_ROWS, d), lambda b, *_: (b, 0)),
            scratch_shapes=[pltpu.SMEM((n_blocks * MOE_ROWS,), I32), pltpu.VMEM((2, MOE_ROWS, d), F32),
                            pltpu.SemaphoreType.DMA((2,))],
        ),
        out_shape=jax.ShapeDtypeStruct((n_blocks * MOE_ROWS, d), BF16),
        compiler_params=_params("arbitrary"),
        name="moe_gather",
    )(e0, e1, r0, r1, blk_start, counts, nused, h)


def _expert_weights_step(owner_ref, start_ref, end_ref, nused, w_hbm, staging, resident, sems, l):
    j = pl.program_id(0)
    b = pl.program_id(1)
    e = owner_ref[b]
    tile = resident[0].shape[1]

    def fetch(jj, ee):
        cols = pl.ds(pl.multiple_of(jj * tile, tile), tile)
        return [pltpu.make_async_copy(w.at[l, ee, :, cols], st, sems.at[n])
                for n, (w, st) in enumerate(zip(w_hbm, staging))]

    @pl.when((b < nused) & (b == start_ref[e]))
    def _():
        @pl.when((j == 0) & (b == 0))
        def _():
            for c in fetch(j, e):
                c.start()
        for c in fetch(j, e):
            c.wait()
        chunk = _pick(resident[0].shape[0], (256, 128, 16))

        def cast(n, c):
            rows = pl.ds(pl.multiple_of(n * chunk, chunk), chunk)
            for st, res in zip(staging, resident):
                res[rows, :] = st[rows, :].astype(BF16)
            return c
        lax.fori_loop(0, resident[0].shape[0] // chunk, cast, 0)
        nxt = end_ref[e]

        @pl.when(nxt < nused)
        def _():
            for c in fetch(j, owner_ref[jnp.minimum(nxt, owner_ref.shape[0] - 1)]):
                c.start()

        @pl.when((nxt >= nused) & (j + 1 < pl.num_programs(0)))
        def _():
            for c in fetch(j + 1, owner_ref[0]):
                c.start()


def _expert_up_kernel(owner_ref, start_ref, end_ref, nused_ref, x_ref, wg_hbm, wu_hbm, o_ref,
                      wg_st, wu_st, wg_b, wu_b, sems, *, l):
    b = pl.program_id(1)
    nused = nused_ref[0]
    _expert_weights_step(owner_ref, start_ref, end_ref, nused, (wg_hbm, wu_hbm), (wg_st, wu_st),
                         (wg_b, wu_b), sems, l)

    @pl.when(b < nused)
    def _():
        x = x_ref[...]
        g = _dot(x, wg_b[...])
        u = _dot(x, wu_b[...])
        o_ref[...] = (g * _sigmoid(g) * u).astype(BF16)

    @pl.when(b >= nused)
    def _():
        o_ref[...] = jnp.zeros_like(o_ref)


def _expert_up(xs, w_gate, w_up, l, owner, blk_start, blk_end, nused):
    n_slots, d = xs.shape
    f = w_gate.shape[-1]
    tf = _pick(f, (512, 256, 128))
    n_blocks = n_slots // MOE_ROWS
    hbm = pl.BlockSpec(memory_space=pl.ANY)
    return pl.pallas_call(
        functools.partial(_expert_up_kernel, l=l),
        grid_spec=pltpu.PrefetchScalarGridSpec(
            num_scalar_prefetch=4,
            grid=(f // tf, n_blocks),
            in_specs=[pl.BlockSpec((MOE_ROWS, d), lambda j, b, *_: (b, 0)), hbm, hbm],
            out_specs=pl.BlockSpec((MOE_ROWS, tf), lambda j, b, *_: (b, j)),
            scratch_shapes=[pltpu.VMEM((d, tf), F32), pltpu.VMEM((d, tf), F32),
                            pltpu.VMEM((d, tf), BF16), pltpu.VMEM((d, tf), BF16),
                            pltpu.SemaphoreType.DMA((2,))],
        ),
        out_shape=jax.ShapeDtypeStruct((n_slots, f), BF16),
        compiler_params=_params("arbitrary", "arbitrary"),
        name="expert_up",
    )(owner, blk_start, blk_end, nused, xs, w_gate, w_up)


def _expert_down_kernel(owner_ref, start_ref, end_ref, nused_ref, x_ref, w_hbm, o_ref, w_st, w_b, sems, *, l):
    b = pl.program_id(1)
    nused = nused_ref[0]
    _expert_weights_step(owner_ref, start_ref, end_ref, nused, (w_hbm,), (w_st,), (w_b,), sems, l)

    @pl.when(b < nused)
    def _():
        o_ref[...] = _dot(x_ref[...], w_b[...])

    @pl.when(b >= nused)
    def _():
        o_ref[...] = jnp.zeros_like(o_ref)


def _expert_down(hmid, w_down, l, owner, blk_start, blk_end, nused):
    n_slots, f = hmid.shape
    d = w_down.shape[-1]
    tn = _pick(d, (1024, 512, 256, 128))
    n_blocks = n_slots // MOE_ROWS
    return pl.pallas_call(
        functools.partial(_expert_down_kernel, l=l),
        grid_spec=pltpu.PrefetchScalarGridSpec(
            num_scalar_prefetch=4,
            grid=(d // tn, n_blocks),
            in_specs=[pl.BlockSpec((MOE_ROWS, f), lambda j, b, *_: (b, 0)),
                      pl.BlockSpec(memory_space=pl.ANY)],
            out_specs=pl.BlockSpec((MOE_ROWS, tn), lambda j, b, *_: (b, j)),
            scratch_shapes=[pltpu.VMEM((f, tn), F32), pltpu.VMEM((f, tn), BF16),
                            pltpu.SemaphoreType.DMA((1,))],
        ),
        out_shape=jax.ShapeDtypeStruct((n_slots, d), F32),
        compiler_params=_params("arbitrary", "arbitrary"),
        name="expert_down",
    )(owner, blk_start, blk_end, nused, hmid, w_down)


def _combine_kernel(e0_ref, e1_ref, r0_ref, r1_ref, start_ref, h_ref, w_ref, g_ref, b_ref, y_ref,
                    o_ref, ob_ref, buf_ref, sem, *, alpha):
    i = pl.program_id(0)
    tb = h_ref.shape[0]

    def copies(r):
        t = i * tb + r
        s0 = start_ref[e0_ref[t]] * MOE_ROWS + r0_ref[t]
        s1 = start_ref[e1_ref[t]] * MOE_ROWS + r1_ref[t]
        return (pltpu.make_async_copy(y_ref.at[pl.ds(s0, 1), :], buf_ref.at[0, pl.ds(r, 1), :], sem),
                pltpu.make_async_copy(y_ref.at[pl.ds(s1, 1), :], buf_ref.at[1, pl.ds(r, 1), :], sem))

    def start(r, c):
        c0, c1 = copies(r)
        c0.start()
        c1.start()
        return c
    lax.fori_loop(0, tb, start, 0, unroll=8)

    def wait(r, c):
        c0, c1 = copies(r)
        c0.wait()
        c1.wait()
        return c
    lax.fori_loop(0, tb, wait, 0, unroll=8)

    w = w_ref[...]
    f = w[:, 0:1] * buf_ref[0] + w[:, 1:2] * buf_ref[1]
    out = _layer_norm_rows(alpha * h_ref[...] + f, g_ref[...], b_ref[...])
    o_ref[...] = out
    ob_ref[...] = out.astype(BF16)


def _combine(h, tok_w, g, b, y, e0, e1, r0, r1, blk_start, alpha):
    m, d = h.shape
    tb = 128
    row = lambda i, *_: (i, 0)
    vec = pl.BlockSpec((1, d), lambda i, *_: (0, 0))
    return pl.pallas_call(
        functools.partial(_combine_kernel, alpha=alpha),
        grid_spec=pltpu.PrefetchScalarGridSpec(
            num_scalar_prefetch=5,
            grid=(m // tb,),
            in_specs=[pl.BlockSpec((tb, d), row), pl.BlockSpec((tb, LANES), row), vec, vec,
                      pl.BlockSpec(memory_space=pl.ANY)],
            out_specs=[pl.BlockSpec((tb, d), row), pl.BlockSpec((tb, d), row)],
            scratch_shapes=[pltpu.VMEM((2, tb, d), F32), pltpu.SemaphoreType.DMA],
        ),
        out_shape=[jax.ShapeDtypeStruct((m, d), F32), jax.ShapeDtypeStruct((m, d), BF16)],
        compiler_params=_params("arbitrary"),
        name="moe_combine_ln",
    )(e0, e1, r0, r1, blk_start, h, tok_w, g.reshape(1, d), b.reshape(1, d), y)


def _sb_prompt_kernel(bias_ref, q_ref, k_ref, v_ref, o_ref, *, scale, nh):
    hblk = pl.program_id(1)
    i = pl.program_id(2)
    tq = q_ref.shape[0]
    row = lax.broadcasted_iota(I32, (tq, tq), 0)
    col = lax.broadcasted_iota(I32, (tq, tq), 1)
    later = (row > col).astype(BF16)
    below_diag = col < row

    def lanes(hh):
        return slice(hh * HEAD_DIM, (hh + 1) * HEAD_DIM)

    qs = [q_ref[:, lanes(hh)].astype(BF16) for hh in range(nh)]
    biases = [bias_ref[hblk * nh + hh] for hh in range(nh)]

    def key_block(j, carry, valid):
        rows = pl.ds(pl.multiple_of(j * tq, tq), tq)
        heads_ = range(nh)
        zs = [_dot_nt(qs[hh], k_ref[rows, lanes(hh)].astype(BF16)) * scale + biases[hh] for hh in heads_]
        sps = [jnp.maximum(z, 0.0) + jnp.log(1.0 + jnp.exp(-jnp.abs(z))) for z in zs]
        lks = [-sp if valid is None else jnp.where(valid, -sp, 0.0) for sp in sps]
        his = [lk.astype(BF16) for lk in lks]
        los = [(lk - hi.astype(F32)).astype(BF16) for lk, hi in zip(lks, his)]
        afters = [_dot(hi, later) + _dot(lo, later) for hi, lo in zip(his, los)]
        out = []
        for hh in heads_:
            run, acc = carry[2 * hh], carry[2 * hh + 1]
            a = jnp.exp(zs[hh] - sps[hh] + afters[hh] + run)
            if valid is not None:
                a = jnp.where(valid, a, 0.0)
            acc = acc + _dot(a.astype(BF16), v_ref[rows, lanes(hh)].astype(BF16))
            out.extend((run + jnp.sum(lks[hh], axis=-1, keepdims=True), acc))
        return tuple(out)

    zero = (jnp.zeros((tq, 1), F32), jnp.zeros((tq, HEAD_DIM), F32)) * nh
    carry = key_block(i, zero, below_diag)
    carry = lax.fori_loop(1, i + 1, lambda n, c: key_block(i - n, c, None), carry)
    for hh in range(nh):
        o_ref[:, lanes(hh)] = carry[2 * hh + 1].astype(o_ref.dtype)


def _sb_prompt(proj, bias, batch, seq, heads):
    tq = _pick(seq, (256, 128))
    nq = seq // tq
    nh = _pick(heads, (4, 2, 1))
    width = nh * HEAD_DIM
    hblocks = heads // nh
    return pl.pallas_call(
        functools.partial(_sb_prompt_kernel, scale=HEAD_DIM ** -0.5, nh=nh),
        grid_spec=pltpu.PrefetchScalarGridSpec(
            num_scalar_prefetch=1,
            grid=(batch, hblocks, nq),
            in_specs=[pl.BlockSpec((tq, width), lambda b, h, i, *_: (b * nq + i, h)),
                      pl.BlockSpec((seq, width), lambda b, h, i, *_: (b, hblocks + h)),
                      pl.BlockSpec((seq, width), lambda b, h, i, *_: (b, 2 * hblocks + h))],
            out_specs=pl.BlockSpec((tq, width), lambda b, h, i, *_: (b * nq + i, h)),
        ),
        out_shape=jax.ShapeDtypeStruct((batch * seq, heads * HEAD_DIM), BF16),
        compiler_params=_params("arbitrary", "arbitrary", "arbitrary"),
        name="sb_prompt",
    )(bias, proj, proj, proj)


def _sb_sample_kernel(pt_ref, bias_ref, q_ref, kn_ref, vn_ref, *rest, scale, past_len, group):
    k_refs, v_refs = rest[:group], rest[group:2 * group]
    o_ref, run_ref, acc_ref = rest[2 * group:]
    p = pl.program_id(1)
    heads = q_ref.shape[0]
    page = k_refs[0].shape[0]
    width = page * heads
    q = q_ref[...]
    bias = bias_ref[...]

    @pl.when(p == 0)
    def _():
        z = jnp.sum(q * kn_ref[...], axis=-1, keepdims=True) * scale + bias
        valid = (lax.broadcasted_iota(I32, z.shape, 1) + past_len) < past_len
        sp = _softplus(z)
        run_ref[...] = jnp.where(valid, -sp, 0.0)
        acc_ref[...] = jnp.where(valid, jnp.exp(z - sp), 0.0) * vn_ref[...]

    qb = q.astype(BF16)
    lane = lax.broadcasted_iota(I32, (heads, width), 1)
    own = (lane % heads) == lax.broadcasted_iota(I32, (heads, width), 0)
    run = run_ref[...]
    acc = acc_ref[...]
    pages = range(group)
    zs = [_dot_nt(qb, k_refs[g][...].reshape(width, HEAD_DIM).astype(BF16)) * scale + bias for g in pages]
    sps = [jnp.maximum(z, 0.0) + jnp.log(1.0 + jnp.exp(-jnp.abs(z))) for z in zs]
    lks = [jnp.where(own, -sp, 0.0) for sp in sps]
    suffixes = list(lks)
    shift = heads
    while shift < width:
        in_page = lane + shift < width
        suffixes = [s + jnp.where(in_page, pltpu.roll(s, width - shift, axis=1), 0.0) for s in suffixes]
        shift *= 2
    for g in pages:
        a = jnp.where(own, jnp.exp(zs[g] - sps[g] + (suffixes[g] - lks[g]) + run), 0.0)
        acc = acc + _dot(a.astype(BF16), v_refs[g][...].reshape(width, HEAD_DIM).astype(BF16))
        run = run + jnp.sum(lks[g], axis=-1, keepdims=True)
    run_ref[...] = run
    acc_ref[...] = acc

    @pl.when(p == pl.num_programs(1) - 1)
    def _():
        o_ref[...] = acc


def _sb_sample(q, k_new, v_new, cache_k, cache_v, page_table, bias, l):
    nseq, heads, _ = q.shape
    n_pages = page_table.shape[1]
    page = cache_k.shape[2]
    group = _pick(n_pages, (4, 2, 1))

    def page_map(g):
        return lambda b, p, pt: (l, pt[b * n_pages + (n_pages - 1 - (p * group + g))], 0, 0, 0)

    head_blk = pl.BlockSpec((None, heads, HEAD_DIM), lambda b, p, pt: (b, 0, 0))
    page_blks = [pl.BlockSpec((None, None, page, heads, HEAD_DIM), page_map(g)) for g in range(group)]
    return pl.pallas_call(
        functools.partial(_sb_sample_kernel, scale=HEAD_DIM ** -0.5, past_len=n_pages * page, group=group),
        grid_spec=pltpu.PrefetchScalarGridSpec(
            num_scalar_prefetch=1,
            grid=(nseq, n_pages // group),
            in_specs=[pl.BlockSpec((heads, 1), lambda b, p, pt: (0, 0)), head_blk, head_blk, head_blk]
                     + page_blks + page_blks,
            out_specs=head_blk,
            scratch_shapes=[pltpu.VMEM((heads, 1), F32), pltpu.VMEM((heads, HEAD_DIM), F32)],
        ),
        out_shape=jax.ShapeDtypeStruct((nseq, heads, HEAD_DIM), F32),
        compiler_params=_params("arbitrary", "arbitrary"),
        name="sb_sample",
    )(page_table.reshape(-1), bias.reshape(heads, 1), q, k_new, v_new,
      *([cache_k] * group), *([cache_v] * group))


def _hgrn_gates(hq, hf, log_lb, log_1m_lb, one_m_lb):
    a = log_lb
    b = log_1m_lb - _softplus(-hf)
    logf = jnp.maximum(a, b) + jnp.log1p(jnp.exp(-jnp.abs(a - b)))
    k = one_m_lb * _sigmoid(-hf)
    q = hq * _sigmoid(hq) * (HEAD_DIM ** -0.5)
    return q, k, logf


def _head_norm_gate(o, gain, gate):
    return o * lax.rsqrt(jnp.mean(o * o, axis=-1, keepdims=True) + NORM_EPS) * gain * gate


def _hgrn_prompt_kernel(hq_ref, hf_ref, hi_ref, hg_ref, llb_ref, l1m_ref, oml_ref, gain_ref,
                        o_ref, s_ref, cf_ref, st_ref):
    seq = hq_ref.shape[0]
    nh = st_ref.shape[0]
    c = HG_CHUNK
    blk = 128
    r = lax.broadcasted_iota(I32, (blk, blk), 0)
    s = lax.broadcasted_iota(I32, (blk, blk), 1)
    tri = ((r // c == s // c) & (s <= r)).astype(F32)

    def lanes(hh):
        return slice(hh * HEAD_DIM, (hh + 1) * HEAD_DIM)

    def prefix(n, carry):
        rows = pl.ds(pl.multiple_of(n * blk, blk), blk)
        for hh in range(nh):
            _, _, logf = _hgrn_gates(hq_ref[rows, lanes(hh)], hf_ref[rows, lanes(hh)],
                                     llb_ref[hh], l1m_ref[hh], oml_ref[hh])
            cf_ref[rows, lanes(hh)] = _dot_f32(tri, logf)
        return carry
    lax.fori_loop(0, seq // blk, prefix, 0)

    st_ref[...] = jnp.zeros_like(st_ref)
    t_idx = lax.broadcasted_iota(I32, (c, c, HEAD_DIM), 0)
    s_idx = lax.broadcasted_iota(I32, (c, c, HEAD_DIM), 1)
    causal = s_idx <= t_idx

    def step(n, carry):
        rows = pl.ds(pl.multiple_of(n * c, c), c)
        for hh in range(nh):
            q, k, _ = _hgrn_gates(hq_ref[rows, lanes(hh)], hf_ref[rows, lanes(hh)],
                                  llb_ref[hh], l1m_ref[hh], oml_ref[hh])
            v = hi_ref[rows, lanes(hh)]
            cf = cf_ref[rows, lanes(hh)]
            rel = jnp.where(causal, cf[:, None, :] - cf[None, :, :], 0.0)
            w = jnp.where(causal, jnp.exp(rel), 0.0)
            scores = jnp.sum(q[:, None, :] * k[None, :, :] * w, axis=-1, keepdims=True)
            o = jnp.sum(scores * v[None, :, :], axis=1)
            st = st_ref[hh]
            o = o + _dot_nt((q * jnp.exp(cf)).astype(BF16), st.astype(BF16))
            tot = cf[c - 1:c, :]
            kd = (k * jnp.exp(tot - cf)).astype(BF16)
            st_ref[hh] = jnp.exp(tot) * st + _dot_tn(v.astype(BF16), kd)
            g = hg_ref[rows, lanes(hh)]
            o_ref[rows, lanes(hh)] = _head_norm_gate(o, gain_ref[hh], g * _sigmoid(g)).astype(o_ref.dtype)
        return carry
    lax.fori_loop(0, seq // c, step, 0)
    for hh in range(nh):
        s_ref[hh] = st_ref[hh].T


def _hgrn_prompt(proj, col0, log_lb, log_1m_lb, one_m_lb, gain, batch, seq, heads):
    nh = _pick(heads, (4, 2, 1))
    width = nh * HEAD_DIM
    cb = col0 // width

    def col(which):
        return pl.BlockSpec((seq, width), lambda b, h: (b, cb + which * (heads // nh) + h))

    vec = pl.BlockSpec((nh, 1, HEAD_DIM), lambda b, h: (h, 0, 0))
    return pl.pallas_call(
        _hgrn_prompt_kernel,
        grid=(batch, heads // nh),
        in_specs=[col(0), col(1), col(2), col(3), vec, vec, vec, vec],
        out_specs=[pl.BlockSpec((seq, width), lambda b, h: (b, h)),
                   pl.BlockSpec((None, nh, HEAD_DIM, HEAD_DIM), lambda b, h: (b, h, 0, 0))],
        out_shape=[jax.ShapeDtypeStruct((batch * seq, heads * HEAD_DIM), BF16),
                   jax.ShapeDtypeStruct((batch, heads, HEAD_DIM, HEAD_DIM), F32)],
        scratch_shapes=[pltpu.VMEM((seq, width), F32), pltpu.VMEM((nh, HEAD_DIM, HEAD_DIM), F32)],
        compiler_params=_params("arbitrary", "arbitrary"),
        name="hgrn_prompt",
    )(proj, proj, proj, proj, log_lb, log_1m_lb, one_m_lb, gain)


def _to_column(row_vec):
    n = row_vec.shape[1]
    r = lax.broadcasted_iota(I32, (n, n), 0)
    c = lax.broadcasted_iota(I32, (n, n), 1)
    return jnp.sum(jnp.where(r == c, jnp.broadcast_to(row_vec, (n, n)), 0.0), axis=-1, keepdims=True)


def _hgrn_sample_kernel(hq_ref, hf_ref, hi_ref, hg_ref, llb_ref, l1m_ref, oml_ref, gain_ref, s0_ref,
                        o_ref, s_ref):
    q, k, logf = _hgrn_gates(hq_ref[...], hf_ref[...], llb_ref[...], l1m_ref[...], oml_ref[...])
    v = hi_ref[...]
    s_old = s0_ref[...]
    scores = jnp.sum(q * k, axis=-1, keepdims=True)
    qf_col = _to_column(_mxu_round(q * jnp.exp(logf)))
    o = scores * v + jnp.sum(qf_col * _mxu_round(s_old), axis=0, keepdims=True)
    s_ref[...] = _to_column(jnp.exp(logf)) * s_old + _to_column(k) * v
    g = hg_ref[...]
    o_ref[...] = _head_norm_gate(o, gain_ref[...], g * _sigmoid(g))


def _hgrn_sample(parts, log_lb, log_1m_lb, one_m_lb, gain, state, l):
    _, nseq, heads = parts.shape[:3]

    def part(which):
        return pl.BlockSpec((None, None, None, 1, HEAD_DIM), lambda b, h: (which, b, h, 0, 0))

    vec = pl.BlockSpec((None, 1, HEAD_DIM), lambda b, h: (h, 0, 0))
    return pl.pallas_call(
        _hgrn_sample_kernel,
        grid=(nseq, heads),
        in_specs=[part(0), part(1), part(2), part(3), vec, vec, vec, vec,
                  pl.BlockSpec((None, None, None, HEAD_DIM, HEAD_DIM), lambda b, h: (l, b, h, 0, 0))],
        out_specs=[pl.BlockSpec((None, None, 1, HEAD_DIM), lambda b, h: (b, h, 0, 0)),
                   pl.BlockSpec((None, None, HEAD_DIM, HEAD_DIM), lambda b, h: (b, h, 0, 0))],
        out_shape=[jax.ShapeDtypeStruct((nseq, heads, 1, HEAD_DIM), F32),
                   jax.ShapeDtypeStruct((nseq, heads, HEAD_DIM, HEAD_DIM), F32)],
        compiler_params=_params("arbitrary", "arbitrary"),
        name="hgrn_sample",
    )(parts, parts, parts, parts, log_lb, log_1m_lb, one_m_lb, gain, state)


def _mlstm_prompt_kernel(bias_ref, q_ref, k_ref, v_ref, og_ref, gcol_ref, grow_ref, gain_ref,
                         o_ref, c_ref, n_ref, m_ref, cst_ref, nst_ref, mst_ref):
    h = pl.program_id(1)
    heads = pl.num_programs(1)
    seq = q_ref.shape[0]
    c = min(ML_CHUNK, seq)
    b_i = bias_ref[h]
    b_f = bias_ref[heads + h]
    cst_ref[...] = jnp.zeros_like(cst_ref)
    nst_ref[...] = jnp.zeros_like(nst_ref)
    mst_ref[...] = jnp.zeros_like(mst_ref)
    t_idx = lax.broadcasted_iota(I32, (c, c), 0)
    s_idx = lax.broadcasted_iota(I32, (c, c), 1)
    causal = s_idx <= t_idx
    gain = gain_ref[...]

    def step(n, carry):
        off = pl.multiple_of(n * c, c)
        rows = pl.ds(off, c)
        li_col = gcol_ref[0, rows, :] + b_i
        lf_col = -_softplus(-(gcol_ref[1, rows, :] + b_f))
        li_row = grow_ref[0, :, rows] + b_i
        lf_row = -_softplus(-(grow_ref[1, :, rows] + b_f))
        lf_rows = jnp.broadcast_to(lf_row, (c, c))
        lf_cols = jnp.broadcast_to(lf_col, (c, c))
        cf_col = jnp.sum(jnp.where(causal, lf_rows, 0.0), axis=-1, keepdims=True)
        cf_row = jnp.sum(jnp.where(t_idx <= s_idx, lf_cols, 0.0), axis=0, keepdims=True)
        m_old = mst_ref[...]
        logd = jnp.where(causal, cf_col - cf_row + li_row, NEG_INF)
        a_col = cf_col + m_old
        mt = jnp.maximum(a_col, jnp.max(logd, axis=-1, keepdims=True))
        q = q_ref[rows, :].astype(BF16)
        k = (k_ref[rows, :] * (HEAD_DIM ** -0.5))
        v = v_ref[rows, :].astype(BF16)
        w = jnp.where(causal, jnp.exp(logd - mt), 0.0) * _dot_nt(q, k.astype(BF16))
        wi = jnp.exp(a_col - mt)
        cst = cst_ref[...]
        nst = nst_ref[...]
        num = _dot(w.astype(BF16), v) + wi * _dot(q, cst.astype(BF16))
        den = jnp.sum(w, axis=-1, keepdims=True) + wi * jnp.sum(q_ref[rows, :] * nst, axis=-1, keepdims=True)
        hid = num / jnp.maximum(jnp.abs(den), jnp.exp(-mt))
        tot = jnp.sum(lf_row, axis=-1, keepdims=True)
        g_row = tot - cf_row + li_row
        g_col = tot - cf_col + li_col
        m_new = jnp.maximum(tot + m_old, jnp.max(g_row, axis=-1, keepdims=True))
        decay = jnp.exp(tot + m_old - m_new)
        wk = jnp.exp(g_col - m_new) * k
        cst_ref[...] = decay * cst + _dot_tn(wk.astype(BF16), v)
        nst_ref[...] = decay * nst + jnp.sum(wk, axis=0, keepdims=True)
        mst_ref[...] = m_new
        o_ref[rows, :] = _head_norm_gate(hid, gain, _sigmoid(og_ref[rows, :])).astype(o_ref.dtype)
        return carry
    lax.fori_loop(0, seq // c, step, 0)
    c_ref[...] = cst_ref[...]
    n_ref[...] = nst_ref[...]
    m_ref[...] = mst_ref[...]


def _mlstm_prompt(proj, col0, gate_cols, gate_rows, bias, gain, batch, seq, heads):
    cb = col0 // HEAD_DIM

    def col(which):
        return pl.BlockSpec((seq, HEAD_DIM), lambda b, h, *_: (b, cb + which * heads + h))

    return pl.pallas_call(
        _mlstm_prompt_kernel,
        grid_spec=pltpu.PrefetchScalarGridSpec(
            num_scalar_prefetch=1,
            grid=(batch, heads),
            in_specs=[col(0), col(1), col(2), col(3),
                      pl.BlockSpec((None, None, 2, seq, 1), lambda b, h, *_: (b, h, 0, 0, 0)),
                      pl.BlockSpec((None, None, 2, 1, seq), lambda b, h, *_: (b, h, 0, 0, 0)),
                      pl.BlockSpec((None, 1, HEAD_DIM), lambda b, h, *_: (h, 0, 0))],
            out_specs=[pl.BlockSpec((seq, HEAD_DIM), lambda b, h, *_: (b, h)),
                       pl.BlockSpec((None, None, HEAD_DIM, HEAD_DIM), lambda b, h, *_: (b, h, 0, 0)),
                       pl.BlockSpec((None, None, 1, HEAD_DIM), lambda b, h, *_: (b, h, 0, 0)),
                       pl.BlockSpec((None, None, 1, 1), lambda b, h, *_: (b, h, 0, 0))],
            scratch_shapes=[pltpu.VMEM((HEAD_DIM, HEAD_DIM), F32), pltpu.VMEM((1, HEAD_DIM), F32),
                            pltpu.VMEM((1, 1), F32)],
        ),
        out_shape=[jax.ShapeDtypeStruct((batch * seq, heads * HEAD_DIM), BF16),
                   jax.ShapeDtypeStruct((batch, heads, HEAD_DIM, HEAD_DIM), F32),
                   jax.ShapeDtypeStruct((batch, heads, 1, HEAD_DIM), F32),
                   jax.ShapeDtypeStruct((batch, heads, 1, 1), F32)],
        compiler_params=_params("arbitrary", "arbitrary"),
        name="mlstm_prompt",
    )(bias, proj, proj, proj, proj, gate_cols, gate_rows, gain)


def _mlstm_sample_kernel(bias_ref, q_ref, k_ref, v_ref, og_ref, gi_ref, gf_ref, gain_ref,
                         c0_ref, n0_ref, m0_ref, o_ref, c_ref, n_ref, m_ref):
    h = pl.program_id(1)
    heads = pl.num_programs(1)
    li = gi_ref[...] + bias_ref[h]
    lf = -_softplus(-(gf_ref[...] + bias_ref[heads + h]))
    q = q_ref[...]
    k = k_ref[...] * (HEAD_DIM ** -0.5)
    v = v_ref[...]
    c_old, n_old, m_old = c0_ref[...], n0_ref[...], m0_ref[...]
    a = lf + m_old
    mt = jnp.maximum(a, li)
    w = jnp.exp(li - mt) * jnp.sum(q * k, axis=-1, keepdims=True)
    wi = jnp.exp(a - mt)
    num = w * v + wi * jnp.sum(_to_column(_mxu_round(q)) * _mxu_round(c_old), axis=0, keepdims=True)
    den = w + wi * jnp.sum(q * n_old, axis=-1, keepdims=True)
    hid = num / jnp.maximum(jnp.abs(den), jnp.exp(-mt))
    m_new = jnp.maximum(lf + m_old, li)
    decay = jnp.exp(lf + m_old - m_new)
    ws = jnp.exp(li - m_new)
    c_ref[...] = decay * c_old + _to_column(ws * k) * v
    n_ref[...] = decay * n_old + ws * k
    m_ref[...] = m_new
    o_ref[...] = _head_norm_gate(hid, gain_ref[...], _sigmoid(og_ref[...]))


def _mlstm_sample(parts, gate_i, gate_f, bias, gain, c0, n0, m0, l):
    _, nseq, heads = parts.shape[:3]

    def part(which):
        return pl.BlockSpec((None, None, None, 1, HEAD_DIM), lambda b, h, *_: (which, b, h, 0, 0))

    one = pl.BlockSpec((None, None, 1, 1), lambda b, h, *_: (b, h, 0, 0))
    vec_o = pl.BlockSpec((None, None, 1, HEAD_DIM), lambda b, h, *_: (b, h, 0, 0))
    mat_o = pl.BlockSpec((None, None, HEAD_DIM, HEAD_DIM), lambda b, h, *_: (b, h, 0, 0))
    return pl.pallas_call(
        _mlstm_sample_kernel,
        grid_spec=pltpu.PrefetchScalarGridSpec(
            num_scalar_prefetch=1,
            grid=(nseq, heads),
            in_specs=[part(0), part(1), part(2), part(3), one, one,
                      pl.BlockSpec((None, 1, HEAD_DIM), lambda b, h, *_: (h, 0, 0)),
                      pl.BlockSpec((None, None, None, HEAD_DIM, HEAD_DIM), lambda b, h, *_: (l, b, h, 0, 0)),
                      pl.BlockSpec((None, None, None, 1, HEAD_DIM), lambda b, h, *_: (l, b, h, 0, 0)),
                      pl.BlockSpec((None, None, None, 1, 1), lambda b, h, *_: (l, b, h, 0, 0))],
            out_specs=[vec_o, mat_o, vec_o, one],
        ),
        out_shape=[jax.ShapeDtypeStruct((nseq, heads, 1, HEAD_DIM), F32),
                   jax.ShapeDtypeStruct((nseq, heads, HEAD_DIM, HEAD_DIM), F32),
                   jax.ShapeDtypeStruct((nseq, heads, 1, HEAD_DIM), F32),
                   jax.ShapeDtypeStruct((nseq, heads, 1, 1), F32)],
        compiler_params=_params("arbitrary", "arbitrary"),
        name="mlstm_sample",
    )(bias, parts, parts, parts, parts, gate_i, gate_f, gain, c0, n0, m0)


def _with_tail(prompt_rows, sample_rows):
    ns = sample_rows.shape[0]
    pad = jnp.zeros((TAIL_ROWS - ns, sample_rows.shape[1]), sample_rows.dtype)
    return jnp.concatenate([prompt_rows, sample_rows, pad], axis=0)


def kernel(x_prompt, x_sample, cache_k, cache_v, page_table, state_hgrn, state_mlstm_c, state_mlstm_n,
           state_mlstm_m, w_in, sb_bias, hg_lb_logits, hg_norm, ml_bias_i, ml_bias_f, ml_norm, w_br_sb,
           w_br_hg, w_br_ml, w_out, ln1_g, ln1_b, w_router_group, b_router_group, w_router_expert,
           b_router_expert, w_e_gate, w_e_up, w_e_down, ln2_g, ln2_b):
    batch, seq, d_model = x_prompt.shape
    nseq = x_sample.shape[0]
    depth = w_in.shape[0]
    n_prompt = batch * seq
    sb_heads = sb_bias.shape[1]
    sb_w = sb_heads * HEAD_DIM
    hg_w = hg_norm.shape[1]
    hg_heads = hg_w // HEAD_DIM
    ml_w = ml_norm.shape[1]
    ml_heads = ml_w // HEAD_DIM
    col_hg = 3 * sb_w
    col_ml = col_hg + 4 * hg_w
    col_if = col_ml + 4 * ml_w
    col_gate = col_if + 2 * ml_heads
    assert x_sample.shape[1] == 1 and nseq <= TAIL_ROWS and w_in.shape[2] == col_gate + 3 * d_model
    alpha = (2 * depth) ** 0.25
    n_tok = n_prompt + TAIL_ROWS
    n_blocks = -(-2 * n_tok // MOE_ROWS) + N_EXPERTS
    assert n_blocks <= LANES

    w_in_t = jnp.swapaxes(w_in, 1, 2)
    lb_all = jnp.cumsum(jax.nn.softmax(hg_lb_logits.astype(F32), axis=0), axis=0)
    lower = lb_all - lb_all[0]

    x = _with_tail(x_prompt.reshape(n_prompt, d_model), x_sample.reshape(nseq, d_model))
    xb = x.astype(BF16)
    outs = {name: [] for name in ("kp", "vp", "ks", "vs", "hp", "hs", "cp", "np", "mp", "cs", "ns", "ms")}

    for l in range(depth):
        proj = _matmul_t(xb, w_in_t, l, row0=0, nrows=col_if)
        gate_if = _matmul_t(xb, w_in_t, l, row0=col_if, nrows=LANES)
        gates = _matmul_t(xb, w_in_t, l, row0=col_gate, nrows=3 * d_model, act="sigmoid")

        tail = proj[n_prompt:n_prompt + nseq]
        outs["kp"].append(proj[:n_prompt, sb_w:2 * sb_w].reshape(batch, seq, sb_heads, HEAD_DIM))
        outs["vp"].append(proj[:n_prompt, 2 * sb_w:3 * sb_w].reshape(batch, seq, sb_heads, HEAD_DIM))
        q_s = tail[:, :sb_w].reshape(nseq, sb_heads, HEAD_DIM)
        k_s = tail[:, sb_w:2 * sb_w].reshape(nseq, sb_heads, HEAD_DIM)
        v_s = tail[:, 2 * sb_w:3 * sb_w].reshape(nseq, sb_heads, HEAD_DIM)
        outs["ks"].append(k_s.reshape(nseq, 1, sb_heads, HEAD_DIM))
        outs["vs"].append(v_s.reshape(nseq, 1, sb_heads, HEAD_DIM))

        o_sb_p = _sb_prompt(proj, sb_bias[l], batch, seq, sb_heads)
        o_sb_s = _sb_sample(q_s, k_s, v_s, cache_k, cache_v, page_table, sb_bias[l], l)
        o_sb = _with_tail(o_sb_p, o_sb_s.reshape(nseq, sb_w).astype(BF16))

        lb = lower[l].reshape(hg_heads, 1, HEAD_DIM)
        log_lb, log_1m_lb, one_m_lb = jnp.log(lb), jnp.log1p(-lb), 1.0 - lb
        hg_gain = hg_norm[l].reshape(hg_heads, 1, HEAD_DIM)
        o_hg_p, s_p = _hgrn_prompt(proj, col_hg, log_lb, log_1m_lb, one_m_lb, hg_gain, batch, seq, hg_heads)
        hg_parts = tail[:, col_hg:col_ml].reshape(nseq, 4, hg_heads, 1, HEAD_DIM).swapaxes(0, 1)
        o_hg_s, s_s = _hgrn_sample(hg_parts, log_lb, log_1m_lb, one_m_lb, hg_gain, state_hgrn, l)
        o_hg = _with_tail(o_hg_p, o_hg_s.reshape(nseq, hg_w).astype(BF16))
        outs["hp"].append(s_p)
        outs["hs"].append(s_s)

        ml_bias = jnp.concatenate([ml_bias_i[l], ml_bias_f[l]]).astype(F32)
        ml_gain = ml_norm[l].reshape(ml_heads, 1, HEAD_DIM)
        g_p = gate_if[:n_prompt, :2 * ml_heads].reshape(batch, seq, 2, ml_heads).transpose(0, 3, 2, 1)
        o_ml_p, c_p, n_p, m_p = _mlstm_prompt(proj, col_ml, g_p[..., None], g_p[:, :, :, None, :], ml_bias,
                                              ml_gain, batch, seq, ml_heads)
        ml_parts = tail[:, col_ml:col_if].reshape(nseq, 4, ml_heads, 1, HEAD_DIM).swapaxes(0, 1)
        g_s = gate_if[n_prompt:n_prompt + nseq, :2 * ml_heads].reshape(nseq, 2, ml_heads, 1, 1)
        o_ml_s, c_s, n_s, m_s = _mlstm_sample(
            ml_parts, g_s[:, 0], g_s[:, 1], ml_bias, ml_gain, state_mlstm_c,
            state_mlstm_n.reshape(depth, nseq, ml_heads, 1, HEAD_DIM),
            state_mlstm_m.reshape(depth, nseq, ml_heads, 1, 1), l)
        o_ml = _with_tail(o_ml_p, o_ml_s.reshape(nseq, ml_w).astype(BF16))
        outs["cp"].append(c_p)
        outs["np"].append(n_p.reshape(batch, ml_heads, HEAD_DIM))
        outs["mp"].append(m_p.reshape(batch, ml_heads))
        outs["cs"].append(c_s)
        outs["ns"].append(n_s.reshape(nseq, ml_heads, HEAD_DIM))
        outs["ms"].append(m_s.reshape(nseq, ml_heads))

        merged = _merge(o_sb, o_hg, o_ml, gates, w_br_sb, w_br_hg, w_br_ml, l)
        mix = _matmul(merged, w_out, lead=(l,))
        w_r = jnp.pad(jnp.concatenate([w_router_group[l], w_router_expert[l]], axis=1).astype(F32),
                      ((0, 0), (0, LANES - N_GROUPS - N_EXPERTS)))
        b_r = jnp.pad(jnp.concatenate([b_router_group[l], b_router_expert[l]]).astype(F32),
                      (0, LANES - N_GROUPS - N_EXPERTS)).reshape(1, LANES)
        h, logits = _ln_router(x, mix, ln1_g[l], ln1_b[l], w_r, alpha)

        tok_i, tok_w, counts = _route(logits, b_r)
        info = _layout(counts)
        blk_start = info[0, :N_EXPERTS]
        nused = info[8, N_EXPERTS - 1:N_EXPERTS]
        owner = info[16:16 + n_blocks, 0]
        e0, e1, r0, r1 = tok_i[:, 0], tok_i[:, 1], tok_i[:, 2], tok_i[:, 3]
        xs = _gather_slots(h, e0, e1, r0, r1, blk_start, counts[0, :N_EXPERTS], nused, n_blocks)
        blk_end = info[8, :N_EXPERTS]
        hmid = _expert_up(xs, w_e_gate, w_e_up, l, owner, blk_start, blk_end, nused)
        y = _expert_down(hmid, w_e_down, l, owner, blk_start, blk_end, nused)
        x, xb = _combine(h, tok_w, ln2_g[l], ln2_b[l], y, e0, e1, r0, r1, blk_start, alpha)

    st = {k: jnp.stack(v) for k, v in outs.items()}
    y_prompt = x[:n_prompt].reshape(batch, seq, d_model)
    y_sample = x[n_prompt:n_prompt + nseq].reshape(nseq, 1, d_model)
    return (y_prompt, y_sample, st["kp"], st["vp"], st["ks"], st["vs"], st["hp"], st["hs"],
            st["cp"], st["np"], st["mp"], st["cs"], st["ns"], st["ms"])
```
